```python
import jax, jax.numpy as jnp
from jax import lax
import numpy as np

D_MODEL = 1024
BATCH = 2
SEQ = 16384
DEPTH = 2

N_MEM = 256
MEM_HEADS = 4
MEM_HEAD_DIM = 128
MEM_WIDTH = MEM_HEADS * MEM_HEAD_DIM
GDN_HEADS = 4
GDN_HEAD_DIM = 128
GDN_WIDTH = GDN_HEADS * GDN_HEAD_DIM
GDN_CONV = 4
GDN_CHUNK = 64
CONV_WIDTH = 512
SHORT_CONV = 3
N_BRANCH = 3
N_EXPERTS = 32
TOP_K = 4
D_FF_EXPERT = 1024
SWIGLU_LIMIT = 7.0
SWIGLU_ALPHA = 1.702
MOE_BLOCK = 256
LN_EPS = 1e-5
RMS_EPS = 1e-6
DEEPNORM_ALPHA = (2 * DEPTH) ** 0.25
DEEPNORM_BETA = (8 * DEPTH) ** -0.25

IN_SPLIT_SIZES = (3 * GDN_WIDTH, GDN_HEADS, GDN_HEADS, GDN_WIDTH,
                  CONV_WIDTH, CONV_WIDTH, CONV_WIDTH, MEM_WIDTH, N_BRANCH * D_MODEL)
D_IN_PROJ = 3 * GDN_WIDTH + 2 * GDN_HEADS + GDN_WIDTH + 3 * CONV_WIDTH + MEM_WIDTH + N_BRANCH * D_MODEL

kernel_name = 'hybrid_gdn_shortconv_memxattn_moe_deepnorm'


def layer_norm(x, g, b):
    xf = x.astype(jnp.float32)
    mu = jnp.mean(xf, axis=-1, keepdims=True)
    var = jnp.mean(jnp.square(xf - mu), axis=-1, keepdims=True)
    y = (xf - mu) * lax.rsqrt(var + LN_EPS) * g.astype(jnp.float32) + b.astype(jnp.float32)
    return y.astype(x.dtype)


def l2_normalize(x):
    return x * lax.rsqrt(jnp.sum(jnp.square(x), axis=-1, keepdims=True) + RMS_EPS)


def causal_depthwise_conv(x, w):
    k_w = w.shape[0]
    return lax.conv_general_dilated(
        x, w[:, None, :].astype(x.dtype), window_strides=(1,), padding=[(k_w - 1, 0)],
        dimension_numbers=('NWC', 'WIO', 'NWC'), feature_group_count=x.shape[-1])


def gated_delta_rule_chunked(q, k, v, g, beta):
    b_, h_, t_, dk = q.shape
    dv = v.shape[-1]
    nc = t_ // GDN_CHUNK

    def chunks(t):
        return t.reshape(b_, h_, nc, GDN_CHUNK, *t.shape[3:])

    q, k, v, g, beta = chunks(q), chunks(k), chunks(v), chunks(g), chunks(beta)
    g_cum = jnp.cumsum(g, axis=-1)
    pos = jnp.arange(GDN_CHUNK)
    causal = pos[:, None] >= pos[None, :]
    strict = pos[:, None] > pos[None, :]
    decay = jnp.exp(jnp.where(causal, g_cum[..., :, None] - g_cum[..., None, :], -jnp.inf))
    k_beta = k * beta[..., None]
    kk = jnp.where(strict, jnp.einsum('bhnid,bhnjd->bhnij', k_beta, k) * decay, 0.0)
    rhs = jnp.concatenate([v * beta[..., None], k_beta * jnp.exp(g_cum)[..., None]], axis=-1)
    sol = lax.linalg.triangular_solve(kk + jnp.eye(GDN_CHUNK, dtype=kk.dtype), rhs,
                                      left_side=True, lower=True, unit_diagonal=True)
    u, k_cum = sol[..., :dv], sol[..., dv:]
    qk = jnp.einsum('bhnid,bhnjd->bhnij', q, k) * decay
    q_dec = q * jnp.exp(g_cum)[..., None]
    k_end = k * jnp.exp(g_cum[..., -1:] - g_cum)[..., None]
    chunk_dec = jnp.exp(g_cum[..., -1])

    def step(state, inp):
        u_c, kc_c, qk_c, qd_c, ke_c, cd_c = inp
        v_new = u_c - jnp.einsum('bhcd,bhde->bhce', kc_c, state)
        o_c = jnp.einsum('bhcd,bhde->bhce', qd_c, state) + jnp.einsum('bhij,bhje->bhie', qk_c, v_new)
        state = state * cd_c[..., None, None] + jnp.einsum('bhcd,bhce->bhde', ke_c, v_new)
        return state, o_c

    xs = tuple(jnp.moveaxis(t, 2, 0) for t in (u, k_cum, qk, q_dec, k_end, chunk_dec))
    s0 = jnp.zeros((b_, h_, dk, dv), jnp.float32)
    _, o = lax.scan(step, s0, xs)
    return jnp.moveaxis(o, 0, 2).reshape(b_, h_, t_, dv)


def token_mixer(x, mem, w_in, conv_qkv, conv_mix, a_log, dt_bias, gdn_norm, w_mem_kv, w_br, w_o):
    b_, t_, _ = x.shape
    f32 = jnp.float32
    split_idx = np.cumsum(IN_SPLIT_SIZES)[:-1].tolist()
    proj = x @ w_in
    qkv, a, bt, z, xc, gb, gc, qm, gl = jnp.split(proj, split_idx, axis=-1)

    qkv = jax.nn.silu(causal_depthwise_conv(qkv, conv_qkv)).astype(f32)
    q, k, v = jnp.split(qkv, 3, axis=-1)

    def heads(t):
        return t.reshape(b_, t_, GDN_HEADS, GDN_HEAD_DIM).transpose(0, 2, 1, 3)

    q = l2_normalize(heads(q)) * (GDN_HEAD_DIM ** -0.5)
    k = l2_normalize(heads(k))
    v = heads(v)
    g = -(jnp.exp(a_log.astype(f32)) * jax.nn.softplus(a.astype(f32) + dt_bias.astype(f32)))
    g = g.transpose(0, 2, 1)
    beta = jax.nn.sigmoid(bt.astype(f32)).transpose(0, 2, 1)
    o = gated_delta_rule_chunked(q, k, v, g, beta).transpose(0, 2, 1, 3)
    o = o * lax.rsqrt(jnp.mean(jnp.square(o), axis=-1, keepdims=True) + RMS_EPS) * gdn_norm.astype(f32)
    zz = z.reshape(b_, t_, GDN_HEADS, GDN_HEAD_DIM).astype(f32)
    y_a = (o * jax.nn.silu(zz)).reshape(b_, t_, GDN_WIDTH).astype(x.dtype)

    y_b = gb * causal_depthwise_conv(gc * xc, conv_mix)

    kv = mem @ w_mem_kv
    km, vm = jnp.split(kv, 2, axis=-1)
    qh = qm.reshape(b_, t_, MEM_HEADS, MEM_HEAD_DIM)
    km = km.reshape(b_, -1, MEM_HEADS, MEM_HEAD_DIM)
    vm = vm.reshape(b_, -1, MEM_HEADS, MEM_HEAD_DIM)
    s = jnp.einsum('bthd,bmhd->bhtm', qh, km).astype(f32) * (MEM_HEAD_DIM ** -0.5)
    p = jax.nn.softmax(s, axis=-1).astype(x.dtype)
    y_m = jnp.einsum('bhtm,bmhd->bthd', p, vm).reshape(b_, t_, MEM_WIDTH)

    gates = jax.nn.sigmoid(gl.reshape(b_, t_, N_BRANCH, D_MODEL))
    merged = (gates[:, :, 0] * (y_a @ w_br[0])
              + gates[:, :, 1] * (y_b @ w_br[1])
              + gates[:, :, 2] * (y_m @ w_br[2]))
    return merged @ w_o


def moe_ffn(x, w_router, b_router, w_gate_up, b_gate_up, w_down, b_down):
    b_, t_, d_ = x.shape
    n_tok = b_ * t_
    xt = x.reshape(n_tok, d_)
    logits = (xt @ w_router + b_router).astype(jnp.float32)
    top_logits, top_idx = lax.top_k(logits, TOP_K)
    top_w = jax.nn.softmax(top_logits, axis=-1).astype(x.dtype)
    n_assign = n_tok * TOP_K
    flat_e = top_idx.reshape(n_assign)
    flat_tok = jnp.arange(n_assign, dtype=jnp.int32) // TOP_K
    flat_w = top_w.reshape(n_assign)
    order = jnp.argsort(flat_e)
    sorted_e = flat_e[order]
    counts = jnp.bincount(flat_e, length=N_EXPERTS)
    padded = (counts + MOE_BLOCK - 1) // MOE_BLOCK * MOE_BLOCK
    pad_end = jnp.cumsum(padded)
    pad_start = pad_end - padded
    start = jnp.cumsum(counts) - counts
    dest = pad_start[sorted_e] + jnp.arange(n_assign, dtype=jnp.int32) - start[sorted_e]
    n_blocks = (n_assign + N_EXPERTS * (MOE_BLOCK - 1) + MOE_BLOCK - 1) // MOE_BLOCK
    n_slots = n_blocks * MOE_BLOCK
    slot_tok = jnp.zeros((n_slots,), jnp.int32).at[dest].set(flat_tok[order])
    slot_w = jnp.zeros((n_slots,), x.dtype).at[dest].set(flat_w[order])
    block_e = jnp.minimum(jnp.searchsorted(pad_end, jnp.arange(n_blocks) * MOE_BLOCK, side='right'),
                          N_EXPERTS - 1)

    def expert_block(args):
        tok, wgt, e = args
        xb = xt[tok]
        gu = xb @ w_gate_up[e] + b_gate_up[e]
        gate, up = jnp.split(gu, 2, axis=-1)
        gate = jnp.minimum(gate, SWIGLU_LIMIT)
        up = jnp.clip(up, -SWIGLU_LIMIT, SWIGLU_LIMIT)
        h = (up + 1.0) * (gate * jax.nn.sigmoid(gate * SWIGLU_ALPHA))
        return (h @ w_down[e] + b_down[e]) * wgt[:, None]

    ys = lax.map(expert_block, (slot_tok.reshape(n_blocks, MOE_BLOCK),
                                slot_w.reshape(n_blocks, MOE_BLOCK), block_e))
    out = jax.ops.segment_sum(ys.reshape(n_slots, d_), slot_tok, num_segments=n_tok)
    return out.reshape(b_, t_, d_)


def setup_inputs(seed: int = 0) -> dict:
    key = jax.random.key(seed)
    ks = jax.random.split(key, 24)
    f32 = jnp.float32
    L = DEPTH

    def nrm(k, shape, scale):
        return jax.random.normal(k, shape, f32) * scale

    dt = jnp.exp(jax.random.uniform(ks[7], (L, GDN_HEADS), f32, float(np.log(1e-3)), float(np.log(1e-1))))
    return {
        'x': nrm(ks[0], (BATCH, SEQ, D_MODEL), 1.0),
        'mem': nrm(ks[1], (BATCH, N_MEM, D_MODEL), 1.0),
        'ln0_g': 1.0 + nrm(ks[2], (D_MODEL,), 0.02),
        'ln0_b': nrm(ks[3], (D_MODEL,), 0.02),
        'w_in': nrm(ks[4], (L, D_MODEL, D_IN_PROJ), D_MODEL ** -0.5),
        'conv_qkv': nrm(ks[5], (L, GDN_CONV, 3 * GDN_WIDTH), GDN_CONV ** -0.5),
        'conv_mix': nrm(ks[6], (L, SHORT_CONV, CONV_WIDTH), SHORT_CONV ** -0.5),
        'a_log': jnp.log(jax.random.uniform(ks[8], (L, GDN_HEADS), f32, 1.0, 16.0)),
        'dt_bias': dt + jnp.log(-jnp.expm1(-dt)),
        'gdn_norm': 1.0 + nrm(ks[9], (L, GDN_HEAD_DIM), 0.02),
        'w_mem_kv': nrm(ks[10], (L, D_MODEL, 2 * MEM_WIDTH), D_MODEL ** -0.5),
        'w_br': nrm(ks[11], (L, N_BRANCH, GDN_WIDTH, D_MODEL), GDN_WIDTH ** -0.5),
        'w_o': nrm(ks[12], (L, D_MODEL, D_MODEL), DEEPNORM_BETA * D_MODEL ** -0.5),
        'ln1_g': 1.0 + nrm(ks[13], (L, D_MODEL), 0.02),
        'ln1_b': nrm(ks[14], (L, D_MODEL), 0.02),
        'w_router': nrm(ks[15], (L, D_MODEL, N_EXPERTS), D_MODEL ** -0.5),
        'b_router': nrm(ks[16], (L, N_EXPERTS), 0.01),
        'w_gate_up': nrm(ks[17], (L, N_EXPERTS, D_MODEL, 2 * D_FF_EXPERT), D_MODEL ** -0.5),
        'b_gate_up': nrm(ks[18], (L, N_EXPERTS, 2 * D_FF_EXPERT), 0.01),
        'w_down': nrm(ks[19], (L, N_EXPERTS, D_FF_EXPERT, D_MODEL), DEEPNORM_BETA * D_FF_EXPERT ** -0.5),
        'b_down': nrm(ks[20], (L, N_EXPERTS, D_MODEL), 0.01),
        'ln2_g': 1.0 + nrm(ks[21], (L, D_MODEL), 0.02),
        'ln2_b': nrm(ks[22], (L, D_MODEL), 0.02),
    }


def reference(x, mem, ln0_g, ln0_b, w_in, conv_qkv, conv_mix, a_log, dt_bias, gdn_norm,
              w_mem_kv, w_br, w_o, ln1_g, ln1_b, w_router, b_router, w_gate_up, b_gate_up,
              w_down, b_down, ln2_g, ln2_b):
    h = layer_norm(x, ln0_g, ln0_b)
    for l in range(DEPTH):
        mix = token_mixer(h, mem, w_in[l], conv_qkv[l], conv_mix[l], a_log[l], dt_bias[l],
                          gdn_norm[l], w_mem_kv[l], w_br[l], w_o[l])
        h = layer_norm(DEEPNORM_ALPHA * h + mix, ln1_g[l], ln1_b[l])
        ffn = moe_ffn(h, w_router[l], b_router[l], w_gate_up[l], b_gate_up[l], w_down[l], b_down[l])
        h = layer_norm(DEEPNORM_ALPHA * h + ffn, ln2_g[l], ln2_b[l])
    return h
```

```python
import functools

import jax
import jax.numpy as jnp
from jax import lax
from jax.experimental import pallas as pl
from jax.experimental.pallas import tpu as pltpu

F32 = jnp.float32
BF16 = jnp.bfloat16
HIGHEST = lax.Precision.HIGHEST

D_MODEL = 1024
N_HEADS = 4
HEAD_DIM = 128
WIDTH = N_HEADS * HEAD_DIM
GDN_CONV = 4
SHORT_CONV = 3
N_EXPERTS = 32
TOP_K = 4
D_FF = 1024
SWIGLU_LIMIT = 7.0
SWIGLU_ALPHA = 1.702
MOE_BLOCK = 256
LN_EPS = 1e-5
RMS_EPS = 1e-6
HALO = 8
LANES = 128
VMEM_LIMIT = 56 * 1024 * 1024

TM_PRE = 256
TM_POST = 256
TM_LN = 512
TT_GDN = 128
GDN_CHUNK = 64


def _dot(a, b, precision=None):
    return jnp.dot(a, b, preferred_element_type=F32, precision=precision)


def _dot_nt(a, b, precision=None):
    return lax.dot_general(a, b, (((1,), (1,)), ((), ())), preferred_element_type=F32,
                           precision=precision)


def _dot_tn(a, b):
    return lax.dot_general(a, b, (((0,), (0,)), ((), ())), preferred_element_type=F32)


def _layer_norm(y, g, b):
    mu = jnp.mean(y, axis=-1, keepdims=True)
    d = y - mu
    var = jnp.mean(d * d, axis=-1, keepdims=True)
    return d * lax.rsqrt(var + LN_EPS) * g + b


def _sigmoid(x):
    return 1.0 / (1.0 + jnp.exp(-x))


def _silu(x):
    return x * _sigmoid(x)


def _softplus(x):
    return jnp.maximum(x, 0.0) + jnp.log(1.0 + jnp.exp(-jnp.abs(x)))


def _const_spec(shape):
    zeros = (0,) * len(shape)
    return pl.BlockSpec(shape, lambda *_: zeros, pipeline_mode=pl.Buffered(1))


def _params(*sem):
    return pltpu.CompilerParams(dimension_semantics=sem, vmem_limit_bytes=VMEM_LIMIT)


def _ln_body(x_ref, g_ref, b_ref, o_ref):
    o_ref[...] = _layer_norm(x_ref[...], g_ref[...], b_ref[...])


def _ln_call(x, g, b):
    n, d = x.shape
    row = pl.BlockSpec((TM_LN, d), lambda i: (i, 0))
    return pl.pallas_call(
        _ln_body, out_shape=jax.ShapeDtypeStruct((n, d), F32), grid=(n // TM_LN,),
        in_specs=[row, _const_spec((1, d)), _const_spec((1, d))], out_specs=row,
        compiler_params=_params("parallel"), name="ln0")(x, g.reshape(1, d), b.reshape(1, d))


def _memkv_body(mem_ref, w_ref, k_ref, v_ref):
    kv = _dot(mem_ref[0].astype(BF16), w_ref[...])
    k_ref[0] = kv[:, :WIDTH].astype(BF16)
    v_ref[0] = kv[:, WIDTH:].astype(BF16)


def _memkv_call(mem, w_kv):
    b, m, d = mem.shape
    out = jax.ShapeDtypeStruct((b, m, WIDTH), BF16)
    blk = pl.BlockSpec((1, m, WIDTH), lambda i: (i, 0, 0))
    return pl.pallas_call(
        _memkv_body, out_shape=(out, out), grid=(b,),
        in_specs=[pl.BlockSpec((1, m, d), lambda i: (i, 0, 0)), _const_spec((d, 2 * WIDTH))],
        out_specs=(blk, blk), compiler_params=_params("parallel"), name="memkv")(mem, w_kv)


def _causal_conv(x, halo_ref, buf_ref, w_ref, first_tile):
    tm = x.shape[0]
    taps = w_ref.shape[0]

    @pl.when(first_tile)
    def _():
        halo_ref[...] = jnp.zeros_like(halo_ref)

    buf_ref[0:HALO, :] = halo_ref[...]
    buf_ref[HALO:HALO + tm, :] = x
    halo_ref[...] = x[tm - HALO:, :]
    acc = w_ref[taps - 1:taps, :] * x
    for j in range(taps - 1):
        off = HALO - (taps - 1) + j
        acc = acc + w_ref[j:j + 1, :] * buf_ref[off:off + tm, :]
    return acc


def _pre_body(h_ref, wq_ref, wk_ref, wv_ref, wab_ref, wabt_ref, wz_ref, wxc_ref, wgb_ref,
              wgc_ref, wqm_ref, wgl_ref, cq_ref, ck_ref, cv_ref, cmix_ref, abp_ref, abpt_ref,
              km_ref, vm_ref, wb1_ref, wb2_ref,
              q_ref, k_ref, v_ref, gcol_ref, grow_ref, zs_ref, part_ref, g0_ref,
              hq_ref, hk_ref, hv_ref, hx_ref, buf_ref, *, tiles_per_seq):
    first = pl.program_id(0) % tiles_per_seq == 0
    hb = h_ref[...].astype(BF16)

    def head_l2(x, scale):
        outs = []
        for hd in range(N_HEADS):
            xh = x[:, hd * HEAD_DIM:(hd + 1) * HEAD_DIM]
            ss = jnp.sum(xh * xh, axis=-1, keepdims=True)
            outs.append(xh * (lax.rsqrt(ss + RMS_EPS) * scale))
        return jnp.concatenate(outs, axis=-1)

    q = _silu(_causal_conv(_dot(hb, wq_ref[...]), hq_ref, buf_ref, cq_ref, first))
    q_ref[...] = head_l2(q, HEAD_DIM ** -0.5)
    k = _silu(_causal_conv(_dot(hb, wk_ref[...]), hk_ref, buf_ref, ck_ref, first))
    k_ref[...] = head_l2(k, 1.0)
    v_ref[...] = _silu(_causal_conv(_dot(hb, wv_ref[...]), hv_ref, buf_ref, cv_ref, first))
    zs_ref[...] = _silu(_dot(hb, wz_ref[...]))

    ab = _dot(hb, wab_ref[...])
    a_scale = abp_ref[0:1, :]
    a_bias = abp_ref[1:2, :]
    lane = lax.broadcasted_iota(jnp.int32, ab.shape, 1)
    gb_col = jnp.where(lane < N_HEADS, a_scale * _softplus(ab + a_bias), _sigmoid(ab))
    gcol_ref[...] = gb_col[:, :2 * N_HEADS]
    abt = _dot_nt(wabt_ref[...], hb)
    sub = lax.broadcasted_iota(jnp.int32, abt.shape, 0)
    grow_ref[...] = jnp.where(sub < N_HEADS,
                              abpt_ref[:, 0:1] * _softplus(abt + abpt_ref[:, 1:2]), _sigmoid(abt))

    xc = _dot(hb, wxc_ref[...])
    gc = _dot(hb, wgc_ref[...])
    y_b = _dot(hb, wgb_ref[...]) * _causal_conv(gc * xc, hx_ref, buf_ref, cmix_ref, first)
    p_b = _dot(y_b.astype(BF16), wb1_ref[...])

    qm = _dot(hb, wqm_ref[...]).astype(BF16)
    heads = []
    for hd in range(N_HEADS):
        sl = slice(hd * HEAD_DIM, (hd + 1) * HEAD_DIM)
        s = _dot_nt(qm[:, sl], km_ref[0, :, sl]) * (HEAD_DIM ** -0.5)
        e = jnp.exp(s - jnp.max(s, axis=-1, keepdims=True))
        p = e / jnp.sum(e, axis=-1, keepdims=True)
        heads.append(_dot(p.astype(BF16), vm_ref[0, :, sl]))
    y_m = jnp.concatenate(heads, axis=-1)
    p_m = _dot(y_m.astype(BF16), wb2_ref[...])

    g0_ref[...] = _sigmoid(_dot(hb, wgl_ref[:, 0:D_MODEL]))
    g1 = _sigmoid(_dot(hb, wgl_ref[:, D_MODEL:2 * D_MODEL]))
    g2 = _sigmoid(_dot(hb, wgl_ref[:, 2 * D_MODEL:3 * D_MODEL]))
    part_ref[...] = g1 * p_b + g2 * p_m


def _pre_call(h, km, vm, w, seq_len):
    n, d = h.shape
    tm = TM_PRE
    tiles_per_seq = seq_len // tm
    row = lambda c: pl.BlockSpec((tm, c), lambda i: (i, 0))
    mem_spec = pl.BlockSpec((1,) + km.shape[1:], lambda i: (i // tiles_per_seq, 0, 0))
    consts = [w["wq"], w["wk"], w["wv"], w["wab"], w["wabt"], w["wz"], w["wxc"], w["wgb"],
              w["wgc"], w["wqm"], w["wgl"], w["cq"], w["ck"], w["cv"], w["cmix"], w["abp"],
              w["abpt"]]
    tail = [w["wb1"], w["wb2"]]
    out_shape = (
        jax.ShapeDtypeStruct((n, WIDTH), F32),
        jax.ShapeDtypeStruct((n, WIDTH), F32),
        jax.ShapeDtypeStruct((n, WIDTH), F32),
        jax.ShapeDtypeStruct((n, 2 * N_HEADS), F32),
        jax.ShapeDtypeStruct((2 * N_HEADS, n), F32),
        jax.ShapeDtypeStruct((n, WIDTH), F32),
        jax.ShapeDtypeStruct((n, d), F32),
        jax.ShapeDtypeStruct((n, d), F32),
    )
    out_specs = (row(WIDTH), row(WIDTH), row(WIDTH), row(2 * N_HEADS),
                 pl.BlockSpec((2 * N_HEADS, tm), lambda i: (0, i)), row(WIDTH), row(d), row(d))
    scratch = [pltpu.VMEM((HALO, WIDTH), F32)] * 4 + [pltpu.VMEM((HALO + tm, WIDTH), F32)]
    return pl.pallas_call(
        functools.partial(_pre_body, tiles_per_seq=tiles_per_seq),
        out_shape=out_shape, grid=(n // tm,),
        in_specs=[row(d)] + [_const_spec(c.shape) for c in consts] + [mem_spec, mem_spec]
        + [_const_spec(c.shape) for c in tail],
        out_specs=out_specs, scratch_shapes=scratch,
        compiler_params=_params("arbitrary"), name="mixer_pre")(h, *consts, km, vm, *tail)


def _gdn_body(q_ref, k_ref, v_ref, gcol_ref, grow_ref, o_ref, s_ref, *, chunk, n_chunks):
    @pl.when(pl.program_id(1) == 0)
    def _():
        s_ref[...] = jnp.zeros_like(s_ref)

    c = chunk
    row = lax.broadcasted_iota(jnp.int32, (c, c), 0)
    col = lax.broadcasted_iota(jnp.int32, (c, c), 1)
    causal = row >= col
    strict = row > col
    lower_ones = causal.astype(F32)
    upper_ones = (row <= col).astype(F32)
    eye = (row == col).astype(F32)
    n_doublings = c.bit_length() - 2

    state = [s_ref[hd] for hd in range(N_HEADS)]
    for ci in range(n_chunks):
        rows = slice(ci * c, (ci + 1) * c)
        gcol = gcol_ref[rows, :]
        grow = grow_ref[:, rows]
        gcum_col = _dot(lower_ones, gcol, precision=HIGHEST)
        gcum_row = _dot(grow, upper_ones, precision=HIGHEST)
        for hd in range(N_HEADS):
            lanes = slice(hd * HEAD_DIM, (hd + 1) * HEAD_DIM)
            q = q_ref[rows, lanes]
            k = k_ref[rows, lanes]
            v = v_ref[rows, lanes]
            gc = gcum_col[:, hd:hd + 1]
            gr = gcum_row[hd:hd + 1, :]
            beta = gcol[:, N_HEADS + hd:N_HEADS + hd + 1]
            g_last = gr[:, c - 1:c]
            decay = jnp.exp(jnp.where(causal, gc - gr, -1e30))
            e_gc = jnp.exp(gc)
            kb = k * beta
            k16 = k.astype(BF16)
            low = jnp.where(strict, _dot_nt(kb.astype(BF16), k16) * decay, 0.0)
            inv = eye - low
            power = low
            for _ in range(n_doublings):
                p16 = power.astype(BF16)
                power = _dot(p16, p16)
                inv = inv + _dot(inv.astype(BF16), power.astype(BF16))
            rhs = jnp.concatenate([v * beta, kb * e_gc], axis=-1).astype(BF16)
            uw = _dot(inv.astype(BF16), rhs)
            u, w_cum = uw[:, :HEAD_DIM], uw[:, HEAD_DIM:]
            qk = _dot_nt(q.astype(BF16), k16) * decay
            q_dec = q * e_gc
            k_end = k * jnp.exp(g_last - gc)
            s_old = state[hd]
            ws_qs = _dot(jnp.concatenate([w_cum, q_dec], axis=0).astype(BF16), s_old.astype(BF16))
            v_new = u - ws_qs[:c]
            v16 = v_new.astype(BF16)
            o_ref[rows, lanes] = ws_qs[c:] + _dot(qk.astype(BF16), v16)
            state[hd] = s_old * jnp.exp(g_last) + _dot_tn(k_end.astype(BF16), v16)
    for hd in range(N_HEADS):
        s_ref[hd] = state[hd]


def _gdn_call(q, k, v, gcol, grow, batch, seq_len):
    n = q.shape[0]
    tt = TT_GDN
    steps = seq_len // tt
    tok = pl.BlockSpec((tt, WIDTH), lambda b, t: (b * steps + t, 0))
    return pl.pallas_call(
        functools.partial(_gdn_body, chunk=GDN_CHUNK, n_chunks=tt // GDN_CHUNK),
        out_shape=jax.ShapeDtypeStruct((n, WIDTH), F32), grid=(batch, steps),
        in_specs=[tok, tok, tok,
                  pl.BlockSpec((tt, 2 * N_HEADS), lambda b, t: (b * steps + t, 0)),
                  pl.BlockSpec((2 * N_HEADS, tt), lambda b, t: (0, b * steps + t))],
        out_specs=tok, scratch_shapes=[pltpu.VMEM((N_HEADS, HEAD_DIM, HEAD_DIM), F32)],
        compiler_params=_params("parallel", "arbitrary"), name="gdn")(q, k, v, gcol, grow)


def _post_body(o_ref, zs_ref, g0_ref, part_ref, h_ref, gn_ref, wb0_ref, wo_ref, lg_ref, lb_ref,
               wr_ref, br_ref, h1_ref, idx_ref, tw_ref, *, alpha):
    o = o_ref[...]
    zs = zs_ref[...]
    heads = []
    for hd in range(N_HEADS):
        sl = slice(hd * HEAD_DIM, (hd + 1) * HEAD_DIM)
        oh = o[:, sl]
        ms = jnp.mean(oh * oh, axis=-1, keepdims=True)
        heads.append(oh * lax.rsqrt(ms + RMS_EPS) * gn_ref[...] * zs[:, sl])
    y_a = jnp.concatenate(heads, axis=-1).astype(BF16)
    merged = g0_ref[...] * _dot(y_a, wb0_ref[...]) + part_ref[...]
    mix = _dot(merged.astype(BF16), wo_ref[...])
    h1 = _layer_norm(alpha * h_ref[...] + mix, lg_ref[...], lb_ref[...])
    h1_ref[...] = h1

    logits = _dot(h1, wr_ref[...], precision=HIGHEST) + br_ref[...]
    lane = lax.broadcasted_iota(jnp.int32, logits.shape, 1)
    out_lane = lax.broadcasted_iota(jnp.int32, (logits.shape[0], TOP_K), 1)
    idx = jnp.zeros((logits.shape[0], TOP_K), jnp.int32)
    top = jnp.zeros((logits.shape[0], TOP_K), F32)
    for j in range(TOP_K):
        m = jnp.max(logits, axis=-1, keepdims=True)
        sel = jnp.min(jnp.where(logits == m, lane, N_EXPERTS), axis=-1, keepdims=True)
        idx = jnp.where(out_lane == j, sel, idx)
        top = jnp.where(out_lane == j, m, top)
        logits = jnp.where(lane == sel, -jnp.inf, logits)
    e = jnp.exp(top - top[:, 0:1])
    idx_ref[...] = idx
    tw_ref[...] = e / jnp.sum(e, axis=-1, keepdims=True)


def _post_call(o, zs, g0, part, h, w, alpha):
    n, d = h.shape
    tm = TM_POST
    row = lambda c: pl.BlockSpec((tm, c), lambda i: (i, 0))
    consts = [w["gn"], w["wb0"], w["wo"], w["ln1_g"], w["ln1_b"], w["wr"], w["br"]]
    return pl.pallas_call(
        functools.partial(_post_body, alpha=alpha),
        out_shape=(jax.ShapeDtypeStruct((n, d), F32),
                   jax.ShapeDtypeStruct((n, TOP_K), jnp.int32),
                   jax.ShapeDtypeStruct((n, TOP_K), F32)),
        grid=(n // tm,),
        in_specs=[row(WIDTH), row(WIDTH), row(d), row(d), row(d)]
        + [_const_spec(c.shape) for c in consts],
        out_specs=(row(d), row(TOP_K), row(TOP_K)),
        compiler_params=_params("parallel"), name="mixer_post")(o, zs, g0, part, h, *consts)


def _route(top_idx, n_blocks):
    n_tok = top_idx.shape[0]
    n_assign = n_tok * TOP_K
    n_slots = n_blocks * MOE_BLOCK
    flat_e = top_idx.reshape(n_assign)
    onehot = (flat_e[:, None] == jnp.arange(N_EXPERTS, dtype=jnp.int32)[None, :]).astype(jnp.int32)
    csum = jnp.cumsum(onehot, axis=0)
    rank = jnp.sum(onehot * csum, axis=1) - 1
    counts = csum[-1]
    padded = (counts + MOE_BLOCK - 1) // MOE_BLOCK * MOE_BLOCK
    pad_end = jnp.cumsum(padded)
    dest = (pad_end - padded)[flat_e] + rank
    slot_a = jnp.full((n_slots,), -1, jnp.int32).at[dest].set(jnp.arange(n_assign, dtype=jnp.int32))
    is_pad = slot_a < 0
    pad_rank = jnp.cumsum(is_pad.astype(jnp.int32)) - 1
    a = jnp.maximum(slot_a, 0)
    slot_tok = a // TOP_K
    slot_dst = jnp.where(is_pad, n_assign + pad_rank, (a % TOP_K) * n_tok + slot_tok)
    block_e = jnp.minimum(
        jnp.searchsorted(pad_end, jnp.arange(n_blocks, dtype=jnp.int32) * MOE_BLOCK, side="right"),
        N_EXPERTS - 1).astype(jnp.int32)
    n_active = (pad_end[-1] // MOE_BLOCK).astype(jnp.int32).reshape(1)
    shape = (n_blocks, 1, MOE_BLOCK)
    return slot_tok.reshape(shape), slot_dst.astype(jnp.int32).reshape(shape), block_e, n_active


def _moe_body(be_ref, nact_ref, tok_ref, dst_ref, h_hbm, wgu_ref, bgu_ref, wd_ref, bd_ref,
              ys_hbm, xbuf, ybuf, gsem, ssem):
    del be_ref
    active = pl.program_id(0) < nact_ref[0]

    @pl.when(active)
    def _():
        def gather(r, carry):
            pltpu.make_async_copy(h_hbm.at[pl.ds(tok_ref[0, 0, r], 1), :],
                                  xbuf.at[pl.ds(r, 1), :], gsem).start()
            return carry

        lax.fori_loop(0, MOE_BLOCK, gather, 0)
        pltpu.make_async_copy(h_hbm.at[pl.ds(0, MOE_BLOCK), :], xbuf, gsem).wait()

        gu = _dot(xbuf[...].astype(BF16), wgu_ref[0]) + bgu_ref[0]
        gate = jnp.minimum(gu[:, :D_FF], SWIGLU_LIMIT)
        up = jnp.clip(gu[:, D_FF:], -SWIGLU_LIMIT, SWIGLU_LIMIT)
        act = (up + 1.0) * (gate * _sigmoid(gate * SWIGLU_ALPHA))
        ybuf[...] = _dot(act.astype(BF16), wd_ref[0]) + bd_ref[0]

    @pl.when(jnp.logical_not(active))
    def _():
        ybuf[...] = jnp.zeros_like(ybuf)

    def scatter(r, carry):
        pltpu.make_async_copy(ybuf.at[pl.ds(r, 1), :],
                              ys_hbm.at[pl.ds(dst_ref[0, 0, r], 1), :], ssem).start()
        return carry

    lax.fori_loop(0, MOE_BLOCK, scatter, 0)
    pltpu.make_async_copy(ybuf, ys_hbm.at[pl.ds(0, MOE_BLOCK), :], ssem).wait()


def _moe_call(h1, slot_tok, slot_dst, block_e, n_active, w):
    n, d = h1.shape
    n_blocks = slot_tok.shape[0]
    smem_row = pl.BlockSpec((1, 1, MOE_BLOCK), lambda i, be, na: (i, 0, 0),
                            memory_space=pltpu.SMEM)
    per_expert = lambda r, c: pl.BlockSpec((1, r, c), lambda i, be, na: (be[i], 0, 0))
    grid_spec = pltpu.PrefetchScalarGridSpec(
        num_scalar_prefetch=2, grid=(n_blocks,),
        in_specs=[smem_row, smem_row, pl.BlockSpec(memory_space=pl.ANY),
                  per_expert(d, 2 * D_FF), per_expert(1, 2 * D_FF),
                  per_expert(D_FF, d), per_expert(1, d)],
        out_specs=pl.BlockSpec(memory_space=pl.ANY),
        scratch_shapes=[pltpu.VMEM((MOE_BLOCK, d), F32), pltpu.VMEM((MOE_BLOCK, d), F32),
                        pltpu.SemaphoreType.DMA, pltpu.SemaphoreType.DMA])
    return pl.pallas_call(
        _moe_body, out_shape=jax.ShapeDtypeStruct((n_blocks * MOE_BLOCK, d), F32),
        grid_spec=grid_spec, compiler_params=_params("arbitrary"), name="moe")(
            block_e, n_active, slot_tok, slot_dst, h1, w["wgu"], w["bgu"], w["wd"], w["bd"])


def _combine_body(y0_ref, y1_ref, y2_ref, y3_ref, tw_ref, h_ref, g_ref, b_ref, o_ref, *, alpha):
    tw = tw_ref[...]
    ffn = tw[:, 0:1] * y0_ref[...]
    for j, y_ref in enumerate((y1_ref, y2_ref, y3_ref), start=1):
        ffn = ffn + tw[:, j:j + 1] * y_ref[...]
    o_ref[...] = _layer_norm(alpha * h_ref[...] + ffn, g_ref[...], b_ref[...])


def _combine_call(ys, top_w, h1, g, b, alpha):
    n, d = h1.shape
    tm = TM_POST
    tiles = n // tm
    row = pl.BlockSpec((tm, d), lambda i: (i, 0))
    y_specs = [pl.BlockSpec((tm, d), lambda i, j=j: (j * tiles + i, 0)) for j in range(TOP_K)]
    return pl.pallas_call(
        functools.partial(_combine_body, alpha=alpha),
        out_shape=jax.ShapeDtypeStruct((n, d), F32), grid=(tiles,),
        in_specs=y_specs + [pl.BlockSpec((tm, TOP_K), lambda i: (i, 0)), row,
                            _const_spec((1, d)), _const_spec((1, d))],
        out_specs=row, compiler_params=_params("parallel"), name="combine")(
            ys, ys, ys, ys, top_w, h1, g, b)


def _layer_weights(l, w_in, conv_qkv, conv_mix, a_log, dt_bias, gdn_norm, w_mem_kv, w_br, w_o,
                   ln1_g, ln1_b, w_router, b_router, w_gate_up, b_gate_up, w_down, b_down,
                   ln2_g, ln2_b):
    d = D_MODEL
    wi = w_in[l].astype(BF16)
    c0 = 3 * WIDTH
    c_z = c0 + 2 * N_HEADS
    c_x = c_z + WIDTH
    c_gl = c_x + 4 * WIDTH
    wab = jnp.zeros((d, LANES), BF16).at[:, :2 * N_HEADS].set(wi[:, c0:c_z])
    neg_a = -jnp.exp(a_log[l].astype(F32))
    dtb = dt_bias[l].astype(F32)
    abp = jnp.zeros((2, LANES), F32).at[0, :N_HEADS].set(neg_a).at[1, :N_HEADS].set(dtb)
    abpt = jnp.zeros((2 * N_HEADS, 2), F32).at[:N_HEADS, 0].set(neg_a).at[:N_HEADS, 1].set(dtb)
    cq = conv_qkv[l].astype(F32)
    return dict(
        wq=wi[:, 0:WIDTH], wk=wi[:, WIDTH:2 * WIDTH], wv=wi[:, 2 * WIDTH:c0],
        wab=wab, wabt=wi[:, c0:c_z].T, wz=wi[:, c_z:c_x],
        wxc=wi[:, c_x:c_x + WIDTH], wgb=wi[:, c_x + WIDTH:c_x + 2 * WIDTH],
        wgc=wi[:, c_x + 2 * WIDTH:c_x + 3 * WIDTH], wqm=wi[:, c_x + 3 * WIDTH:c_gl],
        wgl=wi[:, c_gl:],
        cq=cq[:, 0:WIDTH], ck=cq[:, WIDTH:2 * WIDTH], cv=cq[:, 2 * WIDTH:],
        cmix=conv_mix[l].astype(F32), abp=abp, abpt=abpt,
        wkv=w_mem_kv[l].astype(BF16),
        wb0=w_br[l, 0].astype(BF16), wb1=w_br[l, 1].astype(BF16), wb2=w_br[l, 2].astype(BF16),
        wo=w_o[l].astype(BF16), gn=gdn_norm[l].astype(F32).reshape(1, HEAD_DIM),
        ln1_g=ln1_g[l].reshape(1, d), ln1_b=ln1_b[l].reshape(1, d),
        wr=w_router[l].astype(F32), br=b_router[l].astype(F32).reshape(1, N_EXPERTS),
        wgu=w_gate_up[l].astype(BF16), bgu=b_gate_up[l].reshape(N_EXPERTS, 1, 2 * D_FF),
        wd=w_down[l].astype(BF16), bd=b_down[l].reshape(N_EXPERTS, 1, d),
        ln2_g=ln2_g[l].reshape(1, d), ln2_b=ln2_b[l].reshape(1, d),
    )


def kernel(x, mem, ln0_g, ln0_b, w_in, conv_qkv, conv_mix, a_log, dt_bias, gdn_norm, w_mem_kv, w_br, w_o, ln1_g, ln1_b, w_router, b_router, w_gate_up, b_gate_up, w_down, b_down, ln2_g, ln2_b):
    batch, seq_len, d = x.shape
    depth = w_in.shape[0]
    n_tok = batch * seq_len
    alpha = (2 * depth) ** 0.25
    n_assign = n_tok * TOP_K
    n_blocks = (n_assign + N_EXPERTS * (MOE_BLOCK - 1) + MOE_BLOCK - 1) // MOE_BLOCK
    assert d == D_MODEL and seq_len % TM_PRE == 0 and seq_len % TT_GDN == 0
    assert n_tok % TM_LN == 0 and n_tok % TM_POST == 0

    h = _ln_call(x.reshape(n_tok, d), ln0_g, ln0_b)
    for l in range(depth):
        w = _layer_weights(l, w_in, conv_qkv, conv_mix, a_log, dt_bias, gdn_norm, w_mem_kv, w_br,
                           w_o, ln1_g, ln1_b, w_router, b_router, w_gate_up, b_gate_up, w_down,
                           b_down, ln2_g, ln2_b)
        km, vm = _memkv_call(mem, w["wkv"])
        q, k, v, gcol, grow, zs, part, g0 = _pre_call(h, km, vm, w, seq_len)
        o = _gdn_call(q, k, v, gcol, grow, batch, seq_len)
        h1, top_idx, top_w = _post_call(o, zs, g0, part, h, w, alpha)
        slot_tok, slot_dst, block_e, n_active = _route(top_idx, n_blocks)
        ys = _moe_call(h1, slot_tok, slot_dst, block_e, n_active, w)
        h = _combine_call(ys, top_w, h1, w["ln2_g"], w["ln2_b"], alpha)
    return h.reshape(batch, seq_len, d)
```

```python
import functools

import jax
import jax.numpy as jnp
from jax import lax
from jax.experimental import pallas as pl
from jax.experimental.pallas import tpu as pltpu

F32 = jnp.float32
BF16 = jnp.bfloat16
HIGHEST = lax.Precision.HIGHEST

D_MODEL = 1024
N_HEADS = 4
HEAD_DIM = 128
WIDTH = N_HEADS * HEAD_DIM
GDN_CONV = 4
SHORT_CONV = 3
N_EXPERTS = 32
TOP_K = 4
D_FF = 1024
SWIGLU_LIMIT = 7.0
SWIGLU_ALPHA = 1.702
MOE_BLOCK = 256
N_ROWBUF = 3
LN_EPS = 1e-5
RMS_EPS = 1e-6
HALO = 8
LANES = 128
VMEM_LIMIT = 56 * 1024 * 1024

TM_PRE = 512
TM_POST = 512
TM_LN = 512
TT_GDN = 256
GDN_CHUNK = 64


def _dot(a, b, precision=None):
    return jnp.dot(a, b, preferred_element_type=F32, precision=precision)


def _dot_nt(a, b, precision=None):
    return lax.dot_general(a, b, (((1,), (1,)), ((), ())), preferred_element_type=F32,
                           precision=precision)


def _dot_tn(a, b):
    return lax.dot_general(a, b, (((0,), (0,)), ((), ())), preferred_element_type=F32)


def _layer_norm(y, g, b):
    mu = jnp.mean(y, axis=-1, keepdims=True)
    d = y - mu
    var = jnp.mean(d * d, axis=-1, keepdims=True)
    return d * lax.rsqrt(var + LN_EPS) * g + b


def _sigmoid(x):
    return 1.0 / (1.0 + jnp.exp(-x))


def _silu(x):
    return x * _sigmoid(x)


def _softplus(x):
    return jnp.maximum(x, 0.0) + jnp.log(1.0 + jnp.exp(-jnp.abs(x)))


def _const_spec(shape):
    zeros = (0,) * len(shape)
    return pl.BlockSpec(shape, lambda *_: zeros, pipeline_mode=pl.Buffered(1))


def _params(*sem):
    return pltpu.CompilerParams(dimension_semantics=sem, vmem_limit_bytes=VMEM_LIMIT)


def _ln_body(x_ref, g_ref, b_ref, o_ref):
    o_ref[...] = _layer_norm(x_ref[...], g_ref[...], b_ref[...])


def _ln_call(x, g, b):
    n, d = x.shape
    row = pl.BlockSpec((TM_LN, d), lambda i: (i, 0))
    return pl.pallas_call(
        _ln_body, out_shape=jax.ShapeDtypeStruct((n, d), F32), grid=(n // TM_LN,),
        in_specs=[row, _const_spec((1, d)), _const_spec((1, d))], out_specs=row,
        compiler_params=_params("parallel"), name="ln0")(x, g.reshape(1, d), b.reshape(1, d))


def _memkv_body(mem_ref, w_ref, k_ref, v_ref):
    kv = _dot(mem_ref[0].astype(BF16), w_ref[...])
    k_ref[0] = kv[:, :WIDTH].astype(BF16)
    v_ref[0] = kv[:, WIDTH:].astype(BF16)


def _memkv_call(mem, w_kv):
    b, m, d = mem.shape
    out = jax.ShapeDtypeStruct((b, m, WIDTH), BF16)
    blk = pl.BlockSpec((1, m, WIDTH), lambda i: (i, 0, 0))
    return pl.pallas_call(
        _memkv_body, out_shape=(out, out), grid=(b,),
        in_specs=[pl.BlockSpec((1, m, d), lambda i: (i, 0, 0)), _const_spec((d, 2 * WIDTH))],
        out_specs=(blk, blk), compiler_params=_params("parallel"), name="memkv")(mem, w_kv)


def _causal_conv(x, halo_ref, buf_ref, w_ref, first_tile):
    tm = x.shape[0]
    taps = w_ref.shape[0]

    @pl.when(first_tile)
    def _():
        halo_ref[...] = jnp.zeros_like(halo_ref)

    buf_ref[0:HALO, :] = halo_ref[...]
    buf_ref[HALO:HALO + tm, :] = x
    halo_ref[...] = x[tm - HALO:, :]
    acc = w_ref[taps - 1:taps, :] * x
    for j in range(taps - 1):
        off = HALO - (taps - 1) + j
        acc = acc + w_ref[j:j + 1, :] * buf_ref[off:off + tm, :]
    return acc


def _pre_body(h_ref, wq_ref, wk_ref, wv_ref, wab_ref, wabt_ref, wz_ref, wxc_ref, wgb_ref,
              wgc_ref, wqm_ref, wgl_ref, cq_ref, ck_ref, cv_ref, cmix_ref, abp_ref, abpt_ref,
              km_ref, vm_ref, wb1_ref, wb2_ref,
              q_ref, k_ref, v_ref, gcol_ref, grow_ref, zs_ref, part_ref, g0_ref,
              hq_ref, hk_ref, hv_ref, hx_ref, bufq_ref, bufk_ref, bufv_ref, bufx_ref, *,
              tiles_per_seq):
    first = pl.program_id(0) % tiles_per_seq == 0
    hb = h_ref[...].astype(BF16)

    def head_l2(x, scale):
        outs = []
        for hd in range(N_HEADS):
            xh = x[:, hd * HEAD_DIM:(hd + 1) * HEAD_DIM]
            ss = jnp.sum(xh * xh, axis=-1, keepdims=True)
            outs.append(xh * (lax.rsqrt(ss + RMS_EPS) * scale))
        return jnp.concatenate(outs, axis=-1)

    proj_q = _dot(hb, wq_ref[...])
    proj_k = _dot(hb, wk_ref[...])
    q = _silu(_causal_conv(proj_q, hq_ref, bufq_ref, cq_ref, first))
    q_ref[...] = head_l2(q, HEAD_DIM ** -0.5)
    proj_v = _dot(hb, wv_ref[...])
    k = _silu(_causal_conv(proj_k, hk_ref, bufk_ref, ck_ref, first))
    k_ref[...] = head_l2(k, 1.0)
    proj_z = _dot(hb, wz_ref[...])
    v_ref[...] = _silu(_causal_conv(proj_v, hv_ref, bufv_ref, cv_ref, first))
    xc = _dot(hb, wxc_ref[...])
    gc = _dot(hb, wgc_ref[...])
    zs_ref[...] = _silu(proj_z)
    proj_gb = _dot(hb, wgb_ref[...])

    ab = _dot(hb, wab_ref[...])
    a_scale = abp_ref[0:1, :]
    a_bias = abp_ref[1:2, :]
    lane = lax.broadcasted_iota(jnp.int32, ab.shape, 1)
    gb_col = jnp.where(lane < N_HEADS, a_scale * _softplus(ab + a_bias), _sigmoid(ab))
    gcol_ref[...] = gb_col[:, :2 * N_HEADS]
    abt = _dot_nt(wabt_ref[...], hb)
    sub = lax.broadcasted_iota(jnp.int32, abt.shape, 0)
    grow_ref[...] = jnp.where(sub < N_HEADS,
                              abpt_ref[:, 0:1] * _softplus(abt + abpt_ref[:, 1:2]), _sigmoid(abt))

    qm = _dot(hb, wqm_ref[...]).astype(BF16)
    y_b = proj_gb * _causal_conv(gc * xc, hx_ref, bufx_ref, cmix_ref, first)
    gl0 = _dot(hb, wgl_ref[:, 0:D_MODEL])
    p_b = _dot(y_b.astype(BF16), wb1_ref[...])

    heads = []
    for hd in range(N_HEADS):
        sl = slice(hd * HEAD_DIM, (hd + 1) * HEAD_DIM)
        s = _dot_nt(qm[:, sl], km_ref[0, :, sl]) * (HEAD_DIM ** -0.5)
        e = jnp.exp(s - jnp.max(s, axis=-1, keepdims=True))
        p = e / jnp.sum(e, axis=-1, keepdims=True)
        heads.append(_dot(p.astype(BF16), vm_ref[0, :, sl]))
    y_m = jnp.concatenate(heads, axis=-1)
    gl1 = _dot(hb, wgl_ref[:, D_MODEL:2 * D_MODEL])
    g0_ref[...] = _sigmoid(gl0)

    p_m = _dot(y_m.astype(BF16), wb2_ref[...])
    gl2 = _dot(hb, wgl_ref[:, 2 * D_MODEL:3 * D_MODEL])
    part_ref[...] = _sigmoid(gl1) * p_b + _sigmoid(gl2) * p_m


def _pre_call(h, km, vm, w, seq_len):
    n, d = h.shape
    tm = TM_PRE
    tiles_per_seq = seq_len // tm
    row = lambda c: pl.BlockSpec((tm, c), lambda i: (i, 0))
    mem_spec = pl.BlockSpec((1,) + km.shape[1:], lambda i: (i // tiles_per_seq, 0, 0))
    consts = [w["wq"], w["wk"], w["wv"], w["wab"], w["wabt"], w["wz"], w["wxc"], w["wgb"],
              w["wgc"], w["wqm"], w["wgl"], w["cq"], w["ck"], w["cv"], w["cmix"], w["abp"],
              w["abpt"]]
    tail = [w["wb1"], w["wb2"]]
    out_shape = (
        jax.ShapeDtypeStruct((n, WIDTH), F32),
        jax.ShapeDtypeStruct((n, WIDTH), F32),
        jax.ShapeDtypeStruct((n, WIDTH), F32),
        jax.ShapeDtypeStruct((n, 2 * N_HEADS), F32),
        jax.ShapeDtypeStruct((2 * N_HEADS, n), F32),
        jax.ShapeDtypeStruct((n, WIDTH), F32),
        jax.ShapeDtypeStruct((n, d), F32),
        jax.ShapeDtypeStruct((n, d), F32),
    )
    out_specs = (row(WIDTH), row(WIDTH), row(WIDTH), row(2 * N_HEADS),
                 pl.BlockSpec((2 * N_HEADS, tm), lambda i: (0, i)), row(WIDTH), row(d), row(d))
    scratch = [pltpu.VMEM((HALO, WIDTH), F32)] * 4 + [pltpu.VMEM((HALO + tm, WIDTH), F32)] * 4
    return pl.pallas_call(
        functools.partial(_pre_body, tiles_per_seq=tiles_per_seq),
        out_shape=out_shape, grid=(n // tm,),
        in_specs=[row(d)] + [_const_spec(c.shape) for c in consts] + [mem_spec, mem_spec]
        + [_const_spec(c.shape) for c in tail],
        out_specs=out_specs, scratch_shapes=scratch,
        compiler_params=_params("arbitrary"), name="mixer_pre")(h, *consts, km, vm, *tail)


def _gdn_body(q_ref, k_ref, v_ref, gcol_ref, grow_ref, o_ref, s_ref, *, chunk):
    @pl.when(pl.program_id(1) == 0)
    def _():
        s_ref[...] = jnp.zeros_like(s_ref)

    tt = q_ref.shape[0]
    n_chunks = tt // chunk
    shift = chunk.bit_length() - 1
    row = lax.broadcasted_iota(jnp.int32, (tt, tt), 0)
    col = lax.broadcasted_iota(jnp.int32, (tt, tt), 1)
    same = (row >> shift) == (col >> shift)
    causal = jnp.logical_and(same, row >= col)
    strict = jnp.logical_and(same, row > col)
    eye = (row == col).astype(F32)
    causal_f = causal.astype(F32)
    same_f = same.astype(F32)
    n_doublings = chunk.bit_length() - 2

    gcol = gcol_ref[...]
    grow = grow_ref[...]
    gcum_col = _dot(causal_f, gcol, precision=HIGHEST)
    gcum_row = _dot_nt(grow, causal_f, precision=HIGHEST)
    gtot_col = _dot(same_f, gcol, precision=HIGHEST)
    gtot_row = _dot(grow, same_f, precision=HIGHEST)

    heads = []
    for hd in range(N_HEADS):
        lanes = slice(hd * HEAD_DIM, (hd + 1) * HEAD_DIM)
        k = k_ref[:, lanes]
        gc = gcum_col[:, hd:hd + 1]
        gr = gcum_row[hd:hd + 1, :]
        beta = gcol[:, N_HEADS + hd:N_HEADS + hd + 1]
        decay = jnp.exp(jnp.where(causal, gc - gr, -1e30))
        kb = k * beta
        k16 = k.astype(BF16)
        low = jnp.where(strict, _dot_nt(kb.astype(BF16), k16) * decay, 0.0)
        heads.append(dict(gc=gc, beta=beta, decay=decay, e_gc=jnp.exp(gc), k=k, kb=kb, k16=k16,
                          inv=eye - low, power=low))
    for _ in range(n_doublings):
        for head in heads:
            p16 = head["power"].astype(BF16)
            head["power"] = _dot(p16, p16)
        for head in heads:
            head["inv"] = head["inv"] + _dot(head["inv"].astype(BF16), head["power"].astype(BF16))
    for hd, head in enumerate(heads):
        lanes = slice(hd * HEAD_DIM, (hd + 1) * HEAD_DIM)
        q = q_ref[:, lanes]
        rhs = jnp.concatenate([v_ref[:, lanes] * head["beta"], head["kb"] * head["e_gc"]],
                              axis=-1).astype(BF16)
        uw = _dot(head["inv"].astype(BF16), rhs)
        head.update(u=uw[:, :HEAD_DIM], w=uw[:, HEAD_DIM:], qd=q * head["e_gc"],
                    ke=head["k"] * jnp.exp(gtot_col[:, hd:hd + 1] - head["gc"]),
                    qk=(_dot_nt(q.astype(BF16), head["k16"]) * head["decay"]).astype(BF16))

    pair = 2 * HEAD_DIM
    n_pairs = N_HEADS // 2
    prow = lax.broadcasted_iota(jnp.int32, (pair, pair), 0)
    pcol = lax.broadcasted_iota(jnp.int32, (pair, pair), 1)
    block_diag = (prow >= HEAD_DIM) == (pcol >= HEAD_DIM)
    first_head = lax.broadcasted_iota(jnp.int32, (1, pair), 1) < HEAD_DIM
    two = lambda p, name: jnp.concatenate([heads[2 * p][name], heads[2 * p + 1][name]], axis=-1)
    u2, w2, qd2, ke2 = ([two(p, name) for p in range(n_pairs)] for name in ("u", "w", "qd", "ke"))
    state = [s_ref[p] for p in range(n_pairs)]
    o_state = [[] for _ in range(n_pairs)]
    v_new = [[] for _ in range(n_pairs)]
    for ci in range(n_chunks):
        rows = slice(ci * chunk, (ci + 1) * chunk)
        at = slice(ci * chunk, ci * chunk + 1)
        for p in range(n_pairs):
            lhs = jnp.concatenate([w2[p][rows], qd2[p][rows]], axis=0).astype(BF16)
            res = _dot(lhs, state[p].astype(BF16))
            vn = u2[p][rows] - res[:chunk]
            o_state[p].append(res[chunk:])
            v_new[p].append(vn)
            chunk_decay = jnp.exp(jnp.where(first_head, gtot_row[2 * p:2 * p + 1, at],
                                            gtot_row[2 * p + 1:2 * p + 2, at]))
            outer = _dot_tn(ke2[p][rows].astype(BF16), vn.astype(BF16))
            state[p] = state[p] * chunk_decay + jnp.where(block_diag, outer, 0.0)
    for p in range(n_pairs):
        s_ref[p] = state[p]
        o_pair = jnp.concatenate(o_state[p], axis=0)
        v_pair = jnp.concatenate(v_new[p], axis=0).astype(BF16)
        for j in range(2):
            lanes = slice(j * HEAD_DIM, (j + 1) * HEAD_DIM)
            out_lanes = slice((2 * p + j) * HEAD_DIM, (2 * p + j + 1) * HEAD_DIM)
            o_ref[:, out_lanes] = o_pair[:, lanes] + _dot(heads[2 * p + j]["qk"], v_pair[:, lanes])


def _gdn_call(q, k, v, gcol, grow, batch, seq_len):
    n = q.shape[0]
    tt = TT_GDN
    steps = seq_len // tt
    tok = pl.BlockSpec((tt, WIDTH), lambda b, t: (b * steps + t, 0))
    return pl.pallas_call(
        functools.partial(_gdn_body, chunk=GDN_CHUNK),
        out_shape=jax.ShapeDtypeStruct((n, WIDTH), F32), grid=(batch, steps),
        in_specs=[tok, tok, tok,
                  pl.BlockSpec((tt, 2 * N_HEADS), lambda b, t: (b * steps + t, 0)),
                  pl.BlockSpec((2 * N_HEADS, tt), lambda b, t: (0, b * steps + t))],
        out_specs=tok,
        scratch_shapes=[pltpu.VMEM((N_HEADS // 2, 2 * HEAD_DIM, 2 * HEAD_DIM), F32)],
        compiler_params=_params("parallel", "arbitrary"), name="gdn")(q, k, v, gcol, grow)


def _post_body(o_ref, zs_ref, g0_ref, part_ref, h_ref, gn_ref, wb0_ref, wo_ref, lg_ref, lb_ref,
               wr_ref, br_ref, h1_ref, idx_ref, tw_ref, *, alpha):
    o = o_ref[...]
    zs = zs_ref[...]
    heads = []
    for hd in range(N_HEADS):
        sl = slice(hd * HEAD_DIM, (hd + 1) * HEAD_DIM)
        oh = o[:, sl]
        ms = jnp.mean(oh * oh, axis=-1, keepdims=True)
        heads.append(oh * lax.rsqrt(ms + RMS_EPS) * gn_ref[...] * zs[:, sl])
    y_a = jnp.concatenate(heads, axis=-1).astype(BF16)
    merged = g0_ref[...] * _dot(y_a, wb0_ref[...]) + part_ref[...]
    mix = _dot(merged.astype(BF16), wo_ref[...])
    h1 = _layer_norm(alpha * h_ref[...] + mix, lg_ref[...], lb_ref[...])
    h1_ref[...] = h1

    logits = _dot(h1, wr_ref[...], precision=HIGHEST) + br_ref[...]
    lane = lax.broadcasted_iota(jnp.int32, logits.shape, 1)
    out_lane = lax.broadcasted_iota(jnp.int32, (logits.shape[0], TOP_K), 1)
    idx = jnp.zeros((logits.shape[0], TOP_K), jnp.int32)
    top = jnp.zeros((logits.shape[0], TOP_K), F32)
    for j in range(TOP_K):
        m = jnp.max(logits, axis=-1, keepdims=True)
        sel = jnp.min(jnp.where(logits == m, lane, N_EXPERTS), axis=-1, keepdims=True)
        idx = jnp.where(out_lane == j, sel, idx)
        top = jnp.where(out_lane == j, m, top)
        logits = jnp.where(lane == sel, -jnp.inf, logits)
    e = jnp.exp(top - top[:, 0:1])
    idx_ref[...] = idx
    tw_ref[...] = e / jnp.sum(e, axis=-1, keepdims=True)


def _post_call(o, zs, g0, part, h, w, alpha):
    n, d = h.shape
    tm = TM_POST
    row = lambda c: pl.BlockSpec((tm, c), lambda i: (i, 0))
    consts = [w["gn"], w["wb0"], w["wo"], w["ln1_g"], w["ln1_b"], w["wr"], w["br"]]
    return pl.pallas_call(
        functools.partial(_post_body, alpha=alpha),
        out_shape=(jax.ShapeDtypeStruct((n, d), F32),
                   jax.ShapeDtypeStruct((n, TOP_K), jnp.int32),
                   jax.ShapeDtypeStruct((n, TOP_K), F32)),
        grid=(n // tm,),
        in_specs=[row(WIDTH), row(WIDTH), row(d), row(d), row(d)]
        + [_const_spec(c.shape) for c in consts],
        out_specs=(row(d), row(TOP_K), row(TOP_K)),
        compiler_params=_params("parallel"), name="mixer_post")(o, zs, g0, part, h, *consts)


def _route(top_idx, n_blocks):
    n_tok = top_idx.shape[0]
    n_assign = n_tok * TOP_K
    n_slots = n_blocks * MOE_BLOCK
    flat_e = top_idx.reshape(n_assign)
    seg = 256
    onehot = (flat_e[:, None] == jnp.arange(N_EXPERTS, dtype=jnp.int32)[None, :]).astype(F32)
    oh3 = onehot.reshape(n_assign // seg, seg, N_EXPERTS)
    within = jnp.einsum("ij,bje->bie", jnp.tril(jnp.ones((seg, seg), F32)), oh3)
    seg_tot = within[:, -1, :]
    seg_off = jnp.cumsum(seg_tot, axis=0) - seg_tot
    rank = (jnp.sum(oh3 * (within + seg_off[:, None, :]), axis=-1) - 1.0).astype(jnp.int32)
    rank = rank.reshape(n_assign)
    counts = (seg_off[-1] + seg_tot[-1]).astype(jnp.int32)
    padded = (counts + MOE_BLOCK - 1) // MOE_BLOCK * MOE_BLOCK
    pad_end = jnp.cumsum(padded)
    dest = (pad_end - padded)[flat_e] + rank
    slot_a = jnp.full((n_slots,), -1, jnp.int32).at[dest].set(jnp.arange(n_assign, dtype=jnp.int32))
    is_pad = slot_a < 0
    pad_rank = jnp.cumsum(is_pad.astype(jnp.int32)) - 1
    a = jnp.maximum(slot_a, 0)
    slot_tok = a // TOP_K
    slot_dst = jnp.where(is_pad, n_assign + pad_rank, (a % TOP_K) * n_tok + slot_tok)
    block_e = jnp.minimum(
        jnp.searchsorted(pad_end, jnp.arange(n_blocks + 1, dtype=jnp.int32) * MOE_BLOCK,
                         side="right"), N_EXPERTS - 1).astype(jnp.int32)
    filler_tok = jnp.zeros((3 * MOE_BLOCK,), jnp.int32)
    filler_dst = n_slots + jnp.arange(MOE_BLOCK, dtype=jnp.int32)
    slot_tok = jnp.concatenate([slot_tok.astype(jnp.int32), filler_tok])
    slot_dst = jnp.concatenate([filler_dst, slot_dst.astype(jnp.int32)])
    return (slot_tok.reshape(n_blocks + 3, 1, MOE_BLOCK),
            slot_dst.reshape(n_blocks + 1, 1, MOE_BLOCK), block_e)


def _moe_body(be_ref, tok0_ref, tok1_ref, tok_ref, dst_ref, h_hbm, wgu_ref, bgu_ref, wd_ref,
              bd_ref, ys_hbm, xbuf, ybuf, gsem, ssem, *, n_blocks):
    del be_ref
    i = pl.program_id(0)
    cur = i % N_ROWBUF
    nxt = (i + 2) % N_ROWBUF

    def gather_row(table_ref, r, slot):
        return pltpu.make_async_copy(h_hbm.at[pl.ds(table_ref[0, 0, r], 1), :],
                                     xbuf.at[slot, pl.ds(r, 1), :], gsem.at[slot])

    def gather_wait(slot):
        pltpu.make_async_copy(h_hbm.at[pl.ds(0, MOE_BLOCK), :], xbuf.at[slot],
                              gsem.at[slot]).wait()

    def scatter_wait(slot):
        pltpu.make_async_copy(ybuf.at[slot], ys_hbm.at[pl.ds(0, MOE_BLOCK), :],
                              ssem.at[slot]).wait()

    @pl.when(i == 0)
    def _():
        ybuf[...] = jnp.zeros_like(ybuf)
        for r in range(MOE_BLOCK):
            gather_row(tok0_ref, r, 0).start()
        for r in range(MOE_BLOCK):
            gather_row(tok1_ref, r, 1).start()

    gather_wait(cur)

    @pl.when(i >= 2)
    def _():
        scatter_wait(cur)

    x16 = xbuf[cur].astype(BF16)
    for r in range(MOE_BLOCK):
        pltpu.make_async_copy(ybuf.at[nxt, pl.ds(r, 1), :],
                              ys_hbm.at[pl.ds(dst_ref[0, 0, r], 1), :], ssem.at[nxt]).start()
    for r in range(MOE_BLOCK):
        gather_row(tok_ref, r, nxt).start()
    gu = _dot(x16, wgu_ref[0]) + bgu_ref[0]
    gate = jnp.minimum(gu[:, :D_FF], SWIGLU_LIMIT)
    up = jnp.clip(gu[:, D_FF:], -SWIGLU_LIMIT, SWIGLU_LIMIT)
    act = (up + 1.0) * (gate * _sigmoid(gate * SWIGLU_ALPHA))
    ybuf[cur] = _dot(act.astype(BF16), wd_ref[0]) + bd_ref[0]

    @pl.when(i == n_blocks)
    def _():
        gather_wait((i + 1) % N_ROWBUF)
        gather_wait(nxt)
        scatter_wait((i + 1) % N_ROWBUF)
        scatter_wait(nxt)


def _moe_call(h1, slot_tok, slot_dst, block_e, w):
    n, d = h1.shape
    n_blocks = block_e.shape[0] - 1
    table = lambda f: pl.BlockSpec((1, 1, MOE_BLOCK), f, memory_space=pltpu.SMEM)
    per_expert = lambda r, c: pl.BlockSpec((1, r, c), lambda i, be: (be[i], 0, 0))
    grid_spec = pltpu.PrefetchScalarGridSpec(
        num_scalar_prefetch=1, grid=(n_blocks + 1,),
        in_specs=[table(lambda i, be: (0, 0, 0)), table(lambda i, be: (1, 0, 0)),
                  table(lambda i, be: (i + 2, 0, 0)), table(lambda i, be: (i, 0, 0)),
                  pl.BlockSpec(memory_space=pl.ANY),
                  per_expert(d, 2 * D_FF), per_expert(1, 2 * D_FF),
                  per_expert(D_FF, d), per_expert(1, d)],
        out_specs=pl.BlockSpec(memory_space=pl.ANY),
        scratch_shapes=[pltpu.VMEM((N_ROWBUF, MOE_BLOCK, d), F32),
                        pltpu.VMEM((N_ROWBUF, MOE_BLOCK, d), F32),
                        pltpu.SemaphoreType.DMA((N_ROWBUF,)),
                        pltpu.SemaphoreType.DMA((N_ROWBUF,))])
    return pl.pallas_call(
        functools.partial(_moe_body, n_blocks=n_blocks),
        out_shape=jax.ShapeDtypeStruct(((n_blocks + 1) * MOE_BLOCK, d), F32),
        grid_spec=grid_spec, compiler_params=_params("arbitrary"), name="moe")(
            block_e, slot_tok, slot_tok, slot_tok, slot_dst, h1,
            w["wgu"], w["bgu"], w["wd"], w["bd"])


def _combine_body(y0_ref, y1_ref, y2_ref, y3_ref, tw_ref, h_ref, g_ref, b_ref, o_ref, *, alpha):
    tw = tw_ref[...]
    ffn = tw[:, 0:1] * y0_ref[...]
    for j, y_ref in enumerate((y1_ref, y2_ref, y3_ref), start=1):
        ffn = ffn + tw[:, j:j + 1] * y_ref[...]
    o_ref[...] = _layer_norm(alpha * h_ref[...] + ffn, g_ref[...], b_ref[...])


def _combine_call(ys, top_w, h1, g, b, alpha):
    n, d = h1.shape
    tm = TM_POST
    tiles = n // tm
    row = pl.BlockSpec((tm, d), lambda i: (i, 0))
    y_specs = [pl.BlockSpec((tm, d), lambda i, j=j: (j * tiles + i, 0)) for j in range(TOP_K)]
    return pl.pallas_call(
        functools.partial(_combine_body, alpha=alpha),
        out_shape=jax.ShapeDtypeStruct((n, d), F32), grid=(tiles,),
        in_specs=y_specs + [pl.BlockSpec((tm, TOP_K), lambda i: (i, 0)), row,
                            _const_spec((1, d)), _const_spec((1, d))],
        out_specs=row, compiler_params=_params("parallel"), name="combine")(
            ys, ys, ys, ys, top_w, h1, g, b)


def _layer_weights(l, w_in, conv_qkv, conv_mix, a_log, dt_bias, gdn_norm, w_mem_kv, w_br, w_o,
                   ln1_g, ln1_b, w_router, b_router, w_gate_up, b_gate_up, w_down, b_down,
                   ln2_g, ln2_b):
    d = D_MODEL
    wi = w_in[l].astype(BF16)
    c0 = 3 * WIDTH
    c_z = c0 + 2 * N_HEADS
    c_x = c_z + WIDTH
    c_gl = c_x + 4 * WIDTH
    wab = jnp.zeros((d, LANES), BF16).at[:, :2 * N_HEADS].set(wi[:, c0:c_z])
    neg_a = -jnp.exp(a_log[l].astype(F32))
    dtb = dt_bias[l].astype(F32)
    abp = jnp.zeros((2, LANES), F32).at[0, :N_HEADS].set(neg_a).at[1, :N_HEADS].set(dtb)
    abpt = jnp.zeros((2 * N_HEADS, 2), F32).at[:N_HEADS, 0].set(neg_a).at[:N_HEADS, 1].set(dtb)
    cq = conv_qkv[l].astype(F32)
    return dict(
        wq=wi[:, 0:WIDTH], wk=wi[:, WIDTH:2 * WIDTH], wv=wi[:, 2 * WIDTH:c0],
        wab=wab, wabt=wi[:, c0:c_z].T, wz=wi[:, c_z:c_x],
        wxc=wi[:, c_x:c_x + WIDTH], wgb=wi[:, c_x + WIDTH:c_x + 2 * WIDTH],
        wgc=wi[:, c_x + 2 * WIDTH:c_x + 3 * WIDTH], wqm=wi[:, c_x + 3 * WIDTH:c_gl],
        wgl=wi[:, c_gl:],
        cq=cq[:, 0:WIDTH], ck=cq[:, WIDTH:2 * WIDTH], cv=cq[:, 2 * WIDTH:],
        cmix=conv_mix[l].astype(F32), abp=abp, abpt=abpt,
        wkv=w_mem_kv[l].astype(BF16),
        wb0=w_br[l, 0].astype(BF16), wb1=w_br[l, 1].astype(BF16), wb2=w_br[l, 2].astype(BF16),
        wo=w_o[l].astype(BF16), gn=gdn_norm[l].astype(F32).reshape(1, HEAD_DIM),
        ln1_g=ln1_g[l].reshape(1, d), ln1_b=ln1_b[l].reshape(1, d),
        wr=w_router[l].astype(F32), br=b_router[l].astype(F32).reshape(1, N_EXPERTS),
        wgu=w_gate_up[l].astype(BF16), bgu=b_gate_up[l].reshape(N_EXPERTS, 1, 2 * D_FF),
        wd=w_down[l].astype(BF16), bd=b_down[l].reshape(N_EXPERTS, 1, d),
        ln2_g=ln2_g[l].reshape(1, d), ln2_b=ln2_b[l].reshape(1, d),
    )


def kernel(x, mem, ln0_g, ln0_b, w_in, conv_qkv, conv_mix, a_log, dt_bias, gdn_norm, w_mem_kv, w_br, w_o, ln1_g, ln1_b, w_router, b_router, w_gate_up, b_gate_up, w_down, b_down, ln2_g, ln2_b):
    batch, seq_len, d = x.shape
    depth = w_in.shape[0]
    n_tok = batch * seq_len
    alpha = (2 * depth) ** 0.25
    n_assign = n_tok * TOP_K
    n_blocks = (n_assign + N_EXPERTS * (MOE_BLOCK - 1) + MOE_BLOCK - 1) // MOE_BLOCK
    assert d == D_MODEL and seq_len % TM_PRE == 0 and seq_len % TT_GDN == 0
    assert n_tok % TM_LN == 0 and n_tok % TM_POST == 0

    h = _ln_call(x.reshape(n_tok, d), ln0_g, ln0_b)
    for l in range(depth):
        w = _layer_weights(l, w_in, conv_qkv, conv_mix, a_log, dt_bias, gdn_norm, w_mem_kv, w_br,
                           w_o, ln1_g, ln1_b, w_router, b_router, w_gate_up, b_gate_up, w_down,
                           b_down, ln2_g, ln2_b)
        km, vm = _memkv_call(mem, w["wkv"])
        q, k, v, gcol, grow, zs, part, g0 = _pre_call(h, km, vm, w, seq_len)
        o = _gdn_call(q, k, v, gcol, grow, batch, seq_len)
        h1, top_idx, top_w = _post_call(o, zs, g0, part, h, w, alpha)
        slot_tok, slot_dst, block_e = _route(top_idx, n_blocks)
        ys = _moe_call(h1, slot_tok, slot_dst, block_e, w)
        h = _combine_call(ys, top_w, h1, w["ln2_g"], w["ln2_b"], alpha)
    return h.reshape(batch, seq_len, d)
```

```python
import functools

import jax
import jax.numpy as jnp
from jax import lax
from jax.experimental import pallas as pl
from jax.experimental.pallas import tpu as pltpu

F32 = jnp.float32
BF16 = jnp.bfloat16
HIGHEST = lax.Precision.HIGHEST

D_MODEL = 1024
N_HEADS = 4
HEAD_DIM = 128
WIDTH = N_HEADS * HEAD_DIM
GDN_CONV = 4
SHORT_CONV = 3
N_EXPERTS = 32
TOP_K = 4
D_FF = 1024
SWIGLU_LIMIT = 7.0
SWIGLU_ALPHA = 1.702
MOE_BLOCK = 256
N_ROWBUF = 3
LN_EPS = 1e-5
RMS_EPS = 1e-6
HALO = 8
LANES = 128
VMEM_LIMIT = 56 * 1024 * 1024

TM_PRE = 512
TM_POST = 512
TM_LN = 512
TT_GDN = 256
GDN_CHUNK = 64


def _dot(a, b, precision=None):
    return jnp.dot(a, b, preferred_element_type=F32, precision=precision)


def _dot_nt(a, b, precision=None):
    return lax.dot_general(a, b, (((1,), (1,)), ((), ())), preferred_element_type=F32,
                           precision=precision)


def _dot_tn(a, b):
    return lax.dot_general(a, b, (((0,), (0,)), ((), ())), preferred_element_type=F32)


def _layer_norm(y, g, b):
    mu = jnp.mean(y, axis=-1, keepdims=True)
    d = y - mu
    var = jnp.mean(d * d, axis=-1, keepdims=True)
    return d * lax.rsqrt(var + LN_EPS) * g + b


def _sigmoid(x):
    return 1.0 / (1.0 + jnp.exp(-x))


def _silu(x):
    return x * _sigmoid(x)


def _softplus(x):
    return jnp.maximum(x, 0.0) + jnp.log(1.0 + jnp.exp(-jnp.abs(x)))


def _const_spec(shape):
    zeros = (0,) * len(shape)
    return pl.BlockSpec(shape, lambda *_: zeros, pipeline_mode=pl.Buffered(1))


def _params(*sem):
    return pltpu.CompilerParams(dimension_semantics=sem, vmem_limit_bytes=VMEM_LIMIT)


def _ln_body(x_ref, g_ref, b_ref, o_ref):
    o_ref[...] = _layer_norm(x_ref[...], g_ref[...], b_ref[...])


def _ln_call(x, g, b):
    n, d = x.shape
    row = pl.BlockSpec((TM_LN, d), lambda i: (i, 0))
    return pl.pallas_call(
        _ln_body, out_shape=jax.ShapeDtypeStruct((n, d), F32), grid=(n // TM_LN,),
        in_specs=[row, _const_spec((1, d)), _const_spec((1, d))], out_specs=row,
        compiler_params=_params("parallel"), name="ln0")(x, g.reshape(1, d), b.reshape(1, d))


def _memkv_body(mem_ref, w_ref, k_ref, v_ref):
    kv = _dot(mem_ref[0].astype(BF16), w_ref[...])
    k_ref[0] = kv[:, :WIDTH].astype(BF16)
    v_ref[0] = kv[:, WIDTH:].astype(BF16)


def _memkv_call(mem, w_kv):
    b, m, d = mem.shape
    out = jax.ShapeDtypeStruct((b, m, WIDTH), BF16)
    blk = pl.BlockSpec((1, m, WIDTH), lambda i: (i, 0, 0))
    return pl.pallas_call(
        _memkv_body, out_shape=(out, out), grid=(b,),
        in_specs=[pl.BlockSpec((1, m, d), lambda i: (i, 0, 0)), _const_spec((d, 2 * WIDTH))],
        out_specs=(blk, blk), compiler_params=_params("parallel"), name="memkv")(mem, w_kv)


def _causal_conv(x, halo_ref, buf_ref, w_ref, first_tile):
    tm = x.shape[0]
    taps = w_ref.shape[0]

    @pl.when(first_tile)
    def _():
        halo_ref[...] = jnp.zeros_like(halo_ref)

    buf_ref[0:HALO, :] = halo_ref[...]
    buf_ref[HALO:HALO + tm, :] = x
    halo_ref[...] = x[tm - HALO:, :]
    acc = w_ref[taps - 1:taps, :] * x
    for j in range(taps - 1):
        off = HALO - (taps - 1) + j
        acc = acc + w_ref[j:j + 1, :] * buf_ref[off:off + tm, :]
    return acc


def _pre_body(h_ref, wq_ref, wk_ref, wv_ref, wab_ref, wabt_ref, wz_ref, wxc_ref, wgb_ref,
              wgc_ref, wqm_ref, wgl_ref, cq_ref, ck_ref, cv_ref, cmix_ref, abp_ref, abpt_ref,
              km_ref, vm_ref, wb1_ref, wb2_ref,
              q_ref, k_ref, v_ref, gcol_ref, grow_ref, zs_ref, part_ref, g0_ref,
              hq_ref, hk_ref, hv_ref, hx_ref, bufq_ref, bufk_ref, bufv_ref, bufx_ref, *,
              tiles_per_seq):
    first = pl.program_id(0) % tiles_per_seq == 0
    hb = h_ref[...].astype(BF16)

    def head_l2(x, scale):
        outs = []
        for hd in range(N_HEADS):
            xh = x[:, hd * HEAD_DIM:(hd + 1) * HEAD_DIM]
            ss = jnp.sum(xh * xh, axis=-1, keepdims=True)
            outs.append(xh * (lax.rsqrt(ss + RMS_EPS) * scale))
        return jnp.concatenate(outs, axis=-1)

    proj_q = _dot(hb, wq_ref[...])
    proj_k = _dot(hb, wk_ref[...])
    q = _silu(_causal_conv(proj_q, hq_ref, bufq_ref, cq_ref, first))
    q_ref[...] = head_l2(q, HEAD_DIM ** -0.5)
    proj_v = _dot(hb, wv_ref[...])
    k = _silu(_causal_conv(proj_k, hk_ref, bufk_ref, ck_ref, first))
    k_ref[...] = head_l2(k, 1.0)
    proj_z = _dot(hb, wz_ref[...])
    v_ref[...] = _silu(_causal_conv(proj_v, hv_ref, bufv_ref, cv_ref, first))
    xc = _dot(hb, wxc_ref[...])
    gc = _dot(hb, wgc_ref[...])
    zs_ref[...] = _silu(proj_z)
    proj_gb = _dot(hb, wgb_ref[...])

    ab = _dot(hb, wab_ref[...])
    a_scale = abp_ref[0:1, :]
    a_bias = abp_ref[1:2, :]
    lane = lax.broadcasted_iota(jnp.int32, ab.shape, 1)
    gb_col = jnp.where(lane < N_HEADS, a_scale * _softplus(ab + a_bias), _sigmoid(ab))
    gcol_ref[...] = gb_col[:, :2 * N_HEADS]
    abt = _dot_nt(wabt_ref[...], hb)
    sub = lax.broadcasted_iota(jnp.int32, abt.shape, 0)
    grow_ref[...] = jnp.where(sub < N_HEADS,
                              abpt_ref[:, 0:1] * _softplus(abt + abpt_ref[:, 1:2]), _sigmoid(abt))

    qm = _dot(hb, wqm_ref[...]).astype(BF16)
    y_b = proj_gb * _causal_conv(gc * xc, hx_ref, bufx_ref, cmix_ref, first)
    gl0 = _dot(hb, wgl_ref[:, 0:D_MODEL])
    p_b = _dot(y_b.astype(BF16), wb1_ref[...])

    heads = []
    for hd in range(N_HEADS):
        sl = slice(hd * HEAD_DIM, (hd + 1) * HEAD_DIM)
        s = _dot_nt(qm[:, sl], km_ref[0, :, sl]) * (HEAD_DIM ** -0.5)
        e = jnp.exp(s - jnp.max(s, axis=-1, keepdims=True))
        p = e / jnp.sum(e, axis=-1, keepdims=True)
        heads.append(_dot(p.astype(BF16), vm_ref[0, :, sl]))
    y_m = jnp.concatenate(heads, axis=-1)
    gl1 = _dot(hb, wgl_ref[:, D_MODEL:2 * D_MODEL])
    g0_ref[...] = _sigmoid(gl0)

    p_m = _dot(y_m.astype(BF16), wb2_ref[...])
    gl2 = _dot(hb, wgl_ref[:, 2 * D_MODEL:3 * D_MODEL])
    part_ref[...] = _sigmoid(gl1) * p_b + _sigmoid(gl2) * p_m


def _pre_call(h, km, vm, w, seq_len):
    n, d = h.shape
    tm = TM_PRE
    tiles_per_seq = seq_len // tm
    row = lambda c: pl.BlockSpec((tm, c), lambda i: (i, 0))
    mem_spec = pl.BlockSpec((1,) + km.shape[1:], lambda i: (i // tiles_per_seq, 0, 0))
    consts = [w["wq"], w["wk"], w["wv"], w["wab"], w["wabt"], w["wz"], w["wxc"], w["wgb"],
              w["wgc"], w["wqm"], w["wgl"], w["cq"], w["ck"], w["cv"], w["cmix"], w["abp"],
              w["abpt"]]
    tail = [w["wb1"], w["wb2"]]
    out_shape = (
        jax.ShapeDtypeStruct((n, WIDTH), F32),
        jax.ShapeDtypeStruct((n, WIDTH), F32),
        jax.ShapeDtypeStruct((n, WIDTH), F32),
        jax.ShapeDtypeStruct((n, 2 * N_HEADS), F32),
        jax.ShapeDtypeStruct((2 * N_HEADS, n), F32),
        jax.ShapeDtypeStruct((n, WIDTH), F32),
        jax.ShapeDtypeStruct((n, d), F32),
        jax.ShapeDtypeStruct((n, d), F32),
    )
    out_specs = (row(WIDTH), row(WIDTH), row(WIDTH), row(2 * N_HEADS),
                 pl.BlockSpec((2 * N_HEADS, tm), lambda i: (0, i)), row(WIDTH), row(d), row(d))
    scratch = [pltpu.VMEM((HALO, WIDTH), F32)] * 4 + [pltpu.VMEM((HALO + tm, WIDTH), F32)] * 4
    return pl.pallas_call(
        functools.partial(_pre_body, tiles_per_seq=tiles_per_seq),
        out_shape=out_shape, grid=(n // tm,),
        in_specs=[row(d)] + [_const_spec(c.shape) for c in consts] + [mem_spec, mem_spec]
        + [_const_spec(c.shape) for c in tail],
        out_specs=out_specs, scratch_shapes=scratch,
        compiler_params=_params("arbitrary"), name="mixer_pre")(h, *consts, km, vm, *tail)


def _gdn_body(q_ref, k_ref, v_ref, gcol_ref, grow_ref, o_ref, s_ref, *, chunk):
    @pl.when(pl.program_id(1) == 0)
    def _():
        s_ref[...] = jnp.zeros_like(s_ref)

    tt = q_ref.shape[0]
    n_chunks = tt // chunk
    shift = chunk.bit_length() - 1
    row = lax.broadcasted_iota(jnp.int32, (tt, tt), 0)
    col = lax.broadcasted_iota(jnp.int32, (tt, tt), 1)
    same = (row >> shift) == (col >> shift)
    causal = jnp.logical_and(same, row >= col)
    strict = jnp.logical_and(same, row > col)
    eye = (row == col).astype(F32)
    causal_f = causal.astype(F32)
    same_f = same.astype(F32)
    n_doublings = chunk.bit_length() - 2

    gcol = gcol_ref[...]
    grow = grow_ref[...]
    gcum_col = _dot(causal_f, gcol, precision=HIGHEST)
    gcum_row = _dot_nt(grow, causal_f, precision=HIGHEST)
    gtot_col = _dot(same_f, gcol, precision=HIGHEST)
    gtot_row = _dot(grow, same_f, precision=HIGHEST)

    heads = []
    for hd in range(N_HEADS):
        lanes = slice(hd * HEAD_DIM, (hd + 1) * HEAD_DIM)
        k = k_ref[:, lanes]
        gc = gcum_col[:, hd:hd + 1]
        gr = gcum_row[hd:hd + 1, :]
        beta = gcol[:, N_HEADS + hd:N_HEADS + hd + 1]
        decay = jnp.exp(jnp.where(causal, gc - gr, -1e30))
        kb = k * beta
        k16 = k.astype(BF16)
        low = jnp.where(strict, _dot_nt(kb.astype(BF16), k16) * decay, 0.0)
        heads.append(dict(gc=gc, beta=beta, decay=decay, e_gc=jnp.exp(gc), k=k, kb=kb, k16=k16,
                          inv=eye - low, power=low))
    for _ in range(n_doublings):
        for head in heads:
            p16 = head["power"].astype(BF16)
            head["power"] = _dot(p16, p16)
        for head in heads:
            head["inv"] = head["inv"] + _dot(head["inv"].astype(BF16), head["power"].astype(BF16))
    for hd, head in enumerate(heads):
        lanes = slice(hd * HEAD_DIM, (hd + 1) * HEAD_DIM)
        q = q_ref[:, lanes]
        rhs = jnp.concatenate([v_ref[:, lanes] * head["beta"], head["kb"] * head["e_gc"]],
                              axis=-1).astype(BF16)
        uw = _dot(head["inv"].astype(BF16), rhs)
        head.update(u=uw[:, :HEAD_DIM], w=uw[:, HEAD_DIM:], qd=q * head["e_gc"],
                    ke=head["k"] * jnp.exp(gtot_col[:, hd:hd + 1] - head["gc"]),
                    qk=(_dot_nt(q.astype(BF16), head["k16"]) * head["decay"]).astype(BF16))

    pair = 2 * HEAD_DIM
    n_pairs = N_HEADS // 2
    prow = lax.broadcasted_iota(jnp.int32, (pair, pair), 0)
    pcol = lax.broadcasted_iota(jnp.int32, (pair, pair), 1)
    block_diag = (prow >= HEAD_DIM) == (pcol >= HEAD_DIM)
    first_head = lax.broadcasted_iota(jnp.int32, (1, pair), 1) < HEAD_DIM
    two = lambda p, name: jnp.concatenate([heads[2 * p][name], heads[2 * p + 1][name]], axis=-1)
    u2, w2, qd2, ke2 = ([two(p, name) for p in range(n_pairs)] for name in ("u", "w", "qd", "ke"))
    state = [s_ref[p] for p in range(n_pairs)]
    o_state = [[] for _ in range(n_pairs)]
    v_new = [[] for _ in range(n_pairs)]
    for ci in range(n_chunks):
        rows = slice(ci * chunk, (ci + 1) * chunk)
        at = slice(ci * chunk, ci * chunk + 1)
        for p in range(n_pairs):
            lhs = jnp.concatenate([w2[p][rows], qd2[p][rows]], axis=0).astype(BF16)
            res = _dot(lhs, state[p].astype(BF16))
            vn = u2[p][rows] - res[:chunk]
            o_state[p].append(res[chunk:])
            v_new[p].append(vn)
            chunk_decay = jnp.exp(jnp.where(first_head, gtot_row[2 * p:2 * p + 1, at],
                                            gtot_row[2 * p + 1:2 * p + 2, at]))
            outer = _dot_tn(ke2[p][rows].astype(BF16), vn.astype(BF16))
            state[p] = state[p] * chunk_decay + jnp.where(block_diag, outer, 0.0)
    for p in range(n_pairs):
        s_ref[p] = state[p]
        o_pair = jnp.concatenate(o_state[p], axis=0)
        v_pair = jnp.concatenate(v_new[p], axis=0).astype(BF16)
        for j in range(2):
            lanes = slice(j * HEAD_DIM, (j + 1) * HEAD_DIM)
            out_lanes = slice((2 * p + j) * HEAD_DIM, (2 * p + j + 1) * HEAD_DIM)
            o_ref[:, out_lanes] = o_pair[:, lanes] + _dot(heads[2 * p + j]["qk"], v_pair[:, lanes])


def _gdn_call(q, k, v, gcol, grow, batch, seq_len):
    n = q.shape[0]
    tt = TT_GDN
    steps = seq_len // tt
    tok = pl.BlockSpec((tt, WIDTH), lambda b, t: (b * steps + t, 0))
    return pl.pallas_call(
        functools.partial(_gdn_body, chunk=GDN_CHUNK),
        out_shape=jax.ShapeDtypeStruct((n, WIDTH), F32), grid=(batch, steps),
        in_specs=[tok, tok, tok,
                  pl.BlockSpec((tt, 2 * N_HEADS), lambda b, t: (b * steps + t, 0)),
                  pl.BlockSpec((2 * N_HEADS, tt), lambda b, t: (0, b * steps + t))],
        out_specs=tok,
        scratch_shapes=[pltpu.VMEM((N_HEADS // 2, 2 * HEAD_DIM, 2 * HEAD_DIM), F32)],
        compiler_params=_params("parallel", "arbitrary"), name="gdn")(q, k, v, gcol, grow)


def _post_body(o_ref, zs_ref, g0_ref, part_ref, h_ref, gn_ref, wb0_ref, wo_ref, lg_ref, lb_ref,
               wr_ref, br_ref, h1_ref, h1t_ref, idx_ref, tw_ref, *, alpha):
    o = o_ref[...]
    zs = zs_ref[...]
    heads = []
    for hd in range(N_HEADS):
        sl = slice(hd * HEAD_DIM, (hd + 1) * HEAD_DIM)
        oh = o[:, sl]
        ms = jnp.mean(oh * oh, axis=-1, keepdims=True)
        heads.append(oh * lax.rsqrt(ms + RMS_EPS) * gn_ref[...] * zs[:, sl])
    y_a = jnp.concatenate(heads, axis=-1).astype(BF16)
    merged = g0_ref[...] * _dot(y_a, wb0_ref[...]) + part_ref[...]
    mix = _dot(merged.astype(BF16), wo_ref[...])
    h1 = _layer_norm(alpha * h_ref[...] + mix, lg_ref[...], lb_ref[...])
    h1_ref[...] = h1
    h1t_ref[...] = h1.reshape(h1.shape[0], D_MODEL // LANES, LANES)

    logits = _dot_nt(wr_ref[...], h1, precision=HIGHEST) + br_ref[...]
    expert = lax.broadcasted_iota(jnp.int32, logits.shape, 0)
    idx, top = [], []
    for _ in range(TOP_K):
        m = jnp.max(logits, axis=0, keepdims=True)
        sel = jnp.min(jnp.where(logits == m, expert, N_EXPERTS), axis=0, keepdims=True)
        idx.append(sel)
        top.append(m)
        logits = jnp.where(expert == sel, -jnp.inf, logits)
    e = jnp.exp(jnp.concatenate(top, axis=0) - top[0])
    idx_ref[...] = jnp.concatenate(idx, axis=0)
    tw_ref[...] = e / jnp.sum(e, axis=0, keepdims=True)


def _post_call(o, zs, g0, part, h, w, alpha):
    n, d = h.shape
    tm = TM_POST
    row = lambda c: pl.BlockSpec((tm, c), lambda i: (i, 0))
    consts = [w["gn"], w["wb0"], w["wo"], w["ln1_g"], w["ln1_b"], w["wr"], w["br"]]
    return pl.pallas_call(
        functools.partial(_post_body, alpha=alpha),
        out_shape=(jax.ShapeDtypeStruct((n, d), F32),
                   jax.ShapeDtypeStruct((n, d // LANES, LANES), F32),
                   jax.ShapeDtypeStruct((TOP_K, n), jnp.int32),
                   jax.ShapeDtypeStruct((TOP_K, n), F32)),
        grid=(n // tm,),
        in_specs=[row(WIDTH), row(WIDTH), row(d), row(d), row(d)]
        + [_const_spec(c.shape) for c in consts],
        out_specs=(row(d), pl.BlockSpec((tm, d // LANES, LANES), lambda i: (i, 0, 0)),
                   pl.BlockSpec((TOP_K, tm), lambda i: (0, i)),
                   pl.BlockSpec((TOP_K, tm), lambda i: (0, i))),
        compiler_params=_params("parallel"), name="mixer_post")(o, zs, g0, part, h, *consts)


def _route(top_idx, n_blocks):
    n_tok = top_idx.shape[1]
    n_assign = n_tok * TOP_K
    n_slots = n_blocks * MOE_BLOCK
    flat_e = top_idx.T.reshape(n_assign)
    seg = 256
    onehot = (flat_e[:, None] == jnp.arange(N_EXPERTS, dtype=jnp.int32)[None, :]).astype(F32)
    oh3 = onehot.reshape(n_assign // seg, seg, N_EXPERTS)
    within = jnp.einsum("ij,bje->bie", jnp.tril(jnp.ones((seg, seg), F32)), oh3)
    seg_tot = within[:, -1, :]
    seg_off = jnp.cumsum(seg_tot, axis=0) - seg_tot
    rank = (jnp.sum(oh3 * (within + seg_off[:, None, :]), axis=-1) - 1.0).astype(jnp.int32)
    rank = rank.reshape(n_assign)
    counts = (seg_off[-1] + seg_tot[-1]).astype(jnp.int32)
    padded = (counts + MOE_BLOCK - 1) // MOE_BLOCK * MOE_BLOCK
    pad_end = jnp.cumsum(padded)
    dest = (pad_end - padded)[flat_e] + rank
    slot_a = jnp.full((n_slots,), -1, jnp.int32).at[dest].set(jnp.arange(n_assign, dtype=jnp.int32))
    is_pad = slot_a < 0
    pad_rank = jnp.cumsum(is_pad.astype(jnp.int32)) - 1
    a = jnp.maximum(slot_a, 0)
    slot_tok = a // TOP_K
    slot_dst = jnp.where(is_pad, n_assign + pad_rank, (a % TOP_K) * n_tok + slot_tok)
    block_start = jnp.arange(n_blocks + 1, dtype=jnp.int32) * MOE_BLOCK
    block_e = jnp.minimum(jnp.sum((pad_end[None, :] <= block_start[:, None]).astype(jnp.int32),
                                  axis=1), N_EXPERTS - 1)
    filler_tok = jnp.zeros((3 * MOE_BLOCK,), jnp.int32)
    filler_dst = n_slots + jnp.arange(MOE_BLOCK, dtype=jnp.int32)
    slot_tok = jnp.concatenate([slot_tok.astype(jnp.int32), filler_tok])
    slot_dst = jnp.concatenate([filler_dst, slot_dst.astype(jnp.int32)])
    return (slot_tok.reshape(n_blocks + 3, 1, MOE_BLOCK),
            slot_dst.reshape(n_blocks + 1, 1, MOE_BLOCK), block_e)


def _moe_body(be_ref, tok0_ref, tok1_ref, tok_ref, dst_ref, h_hbm, wgu_ref, bgu_ref, wd_ref,
              bd_ref, ys_hbm, xbuf, ybuf, gsem, ssem, *, n_blocks):
    del be_ref
    i = pl.program_id(0)
    cur = i % N_ROWBUF
    nxt = (i + 2) % N_ROWBUF

    def gather_row(table_ref, r, slot):
        return pltpu.make_async_copy(h_hbm.at[table_ref[0, 0, r]], xbuf.at[slot, r],
                                     gsem.at[slot])

    def gather_wait(slot):
        pltpu.make_async_copy(h_hbm.at[pl.ds(0, MOE_BLOCK)], xbuf.at[slot],
                              gsem.at[slot]).wait()

    def scatter_wait(slot):
        pltpu.make_async_copy(ybuf.at[slot], ys_hbm.at[pl.ds(0, MOE_BLOCK)],
                              ssem.at[slot]).wait()

    @pl.when(i == 0)
    def _():
        ybuf[...] = jnp.zeros_like(ybuf)
        for r in range(MOE_BLOCK):
            gather_row(tok0_ref, r, 0).start()
        for r in range(MOE_BLOCK):
            gather_row(tok1_ref, r, 1).start()

    gather_wait(cur)

    @pl.when(i >= 2)
    def _():
        scatter_wait(cur)

    x16 = xbuf[cur].reshape(MOE_BLOCK, D_MODEL).astype(BF16)
    for r in range(MOE_BLOCK):
        pltpu.make_async_copy(ybuf.at[nxt, r], ys_hbm.at[dst_ref[0, 0, r]], ssem.at[nxt]).start()
    for r in range(MOE_BLOCK):
        gather_row(tok_ref, r, nxt).start()
    gu = _dot(x16, wgu_ref[0]) + bgu_ref[0]
    gate = jnp.minimum(gu[:, :D_FF], SWIGLU_LIMIT)
    up = jnp.clip(gu[:, D_FF:], -SWIGLU_LIMIT, SWIGLU_LIMIT)
    act = (up + 1.0) * (gate * _sigmoid(gate * SWIGLU_ALPHA))
    y = _dot(act.astype(BF16), wd_ref[0]) + bd_ref[0]
    ybuf[cur] = y.reshape(MOE_BLOCK, D_MODEL // LANES, LANES)

    @pl.when(i == n_blocks)
    def _():
        gather_wait((i + 1) % N_ROWBUF)
        gather_wait(nxt)
        scatter_wait((i + 1) % N_ROWBUF)
        scatter_wait(nxt)


def _moe_call(h1t, slot_tok, slot_dst, block_e, w):
    n, sub, lanes = h1t.shape
    d = sub * lanes
    n_blocks = block_e.shape[0] - 1
    table = lambda f: pl.BlockSpec((1, 1, MOE_BLOCK), f, memory_space=pltpu.SMEM)
    per_expert = lambda r, c: pl.BlockSpec((1, r, c), lambda i, be: (be[i], 0, 0))
    grid_spec = pltpu.PrefetchScalarGridSpec(
        num_scalar_prefetch=1, grid=(n_blocks + 1,),
        in_specs=[table(lambda i, be: (0, 0, 0)), table(lambda i, be: (1, 0, 0)),
                  table(lambda i, be: (i + 2, 0, 0)), table(lambda i, be: (i, 0, 0)),
                  pl.BlockSpec(memory_space=pl.ANY),
                  per_expert(d, 2 * D_FF), per_expert(1, 2 * D_FF),
                  per_expert(D_FF, d), per_expert(1, d)],
        out_specs=pl.BlockSpec(memory_space=pl.ANY),
        scratch_shapes=[pltpu.VMEM((N_ROWBUF, MOE_BLOCK, sub, lanes), F32),
                        pltpu.VMEM((N_ROWBUF, MOE_BLOCK, sub, lanes), F32),
                        pltpu.SemaphoreType.DMA((N_ROWBUF,)),
                        pltpu.SemaphoreType.DMA((N_ROWBUF,))])
    return pl.pallas_call(
        functools.partial(_moe_body, n_blocks=n_blocks),
        out_shape=jax.ShapeDtypeStruct(((n_blocks + 1) * MOE_BLOCK, sub, lanes), F32),
        grid_spec=grid_spec, compiler_params=_params("arbitrary"), name="moe")(
            block_e, slot_tok, slot_tok, slot_tok, slot_dst, h1t,
            w["wgu"], w["bgu"], w["wd"], w["bd"])


def _combine_body(y0_ref, y1_ref, y2_ref, y3_ref, tw_ref, h_ref, g_ref, b_ref, o_ref, *, alpha):
    tw = tw_ref[...]
    ffn = None
    for j, y_ref in enumerate((y0_ref, y1_ref, y2_ref, y3_ref)):
        term = tw[:, j:j + 1] * y_ref[...].reshape(h_ref.shape)
        ffn = term if ffn is None else ffn + term
    o_ref[...] = _layer_norm(alpha * h_ref[...] + ffn, g_ref[...], b_ref[...])


def _combine_call(ys, top_w, h1, g, b, alpha):
    n, d = h1.shape
    tm = TM_POST
    tiles = n // tm
    row = pl.BlockSpec((tm, d), lambda i: (i, 0))
    y_specs = [pl.BlockSpec((tm,) + ys.shape[1:], lambda i, j=j: (j * tiles + i, 0, 0))
               for j in range(TOP_K)]
    return pl.pallas_call(
        functools.partial(_combine_body, alpha=alpha),
        out_shape=jax.ShapeDtypeStruct((n, d), F32), grid=(tiles,),
        in_specs=y_specs + [pl.BlockSpec((tm, TOP_K), lambda i: (i, 0)), row,
                            _const_spec((1, d)), _const_spec((1, d))],
        out_specs=row, compiler_params=_params("parallel"), name="combine")(
            ys, ys, ys, ys, top_w, h1, g, b)


def _layer_weights(l, w_in, conv_qkv, conv_mix, a_log, dt_bias, gdn_norm, w_mem_kv, w_br, w_o,
                   ln1_g, ln1_b, w_router, b_router, w_gate_up, b_gate_up, w_down, b_down,
                   ln2_g, ln2_b):
    d = D_MODEL
    wi = w_in[l].astype(BF16)
    c0 = 3 * WIDTH
    c_z = c0 + 2 * N_HEADS
    c_x = c_z + WIDTH
    c_gl = c_x + 4 * WIDTH
    wab = jnp.zeros((d, LANES), BF16).at[:, :2 * N_HEADS].set(wi[:, c0:c_z])
    neg_a = -jnp.exp(a_log[l].astype(F32))
    dtb = dt_bias[l].astype(F32)
    abp = jnp.zeros((2, LANES), F32).at[0, :N_HEADS].set(neg_a).at[1, :N_HEADS].set(dtb)
    abpt = jnp.zeros((2 * N_HEADS, 2), F32).at[:N_HEADS, 0].set(neg_a).at[:N_HEADS, 1].set(dtb)
    cq = conv_qkv[l].astype(F32)
    return dict(
        wq=wi[:, 0:WIDTH], wk=wi[:, WIDTH:2 * WIDTH], wv=wi[:, 2 * WIDTH:c0],
        wab=wab, wabt=wi[:, c0:c_z].T, wz=wi[:, c_z:c_x],
        wxc=wi[:, c_x:c_x + WIDTH], wgb=wi[:, c_x + WIDTH:c_x + 2 * WIDTH],
        wgc=wi[:, c_x + 2 * WIDTH:c_x + 3 * WIDTH], wqm=wi[:, c_x + 3 * WIDTH:c_gl],
        wgl=wi[:, c_gl:],
        cq=cq[:, 0:WIDTH], ck=cq[:, WIDTH:2 * WIDTH], cv=cq[:, 2 * WIDTH:],
        cmix=conv_mix[l].astype(F32), abp=abp, abpt=abpt,
        wkv=w_mem_kv[l].astype(BF16),
        wb0=w_br[l, 0].astype(BF16), wb1=w_br[l, 1].astype(BF16), wb2=w_br[l, 2].astype(BF16),
        wo=w_o[l].astype(BF16), gn=gdn_norm[l].astype(F32).reshape(1, HEAD_DIM),
        ln1_g=ln1_g[l].reshape(1, d), ln1_b=ln1_b[l].reshape(1, d),
        wr=w_router[l].astype(F32).T, br=b_router[l].astype(F32).reshape(N_EXPERTS, 1),
        wgu=w_gate_up[l].astype(BF16), bgu=b_gate_up[l].reshape(N_EXPERTS, 1, 2 * D_FF),
        wd=w_down[l].astype(BF16), bd=b_down[l].reshape(N_EXPERTS, 1, d),
        ln2_g=ln2_g[l].reshape(1, d), ln2_b=ln2_b[l].reshape(1, d),
    )


def kernel(x, mem, ln0_g, ln0_b, w_in, conv_qkv, conv_mix, a_log, dt_bias, gdn_norm, w_mem_kv, w_br, w_o, ln1_g, ln1_b, w_router, b_router, w_gate_up, b_gate_up, w_down, b_down, ln2_g, ln2_b):
    batch, seq_len, d = x.shape
    depth = w_in.shape[0]
    n_tok = batch * seq_len
    alpha = (2 * depth) ** 0.25
    n_assign = n_tok * TOP_K
    n_blocks = (n_assign + N_EXPERTS * (MOE_BLOCK - 1) + MOE_BLOCK - 1) // MOE_BLOCK
    assert d == D_MODEL and seq_len % TM_PRE == 0 and seq_len % TT_GDN == 0
    assert n_tok % TM_LN == 0 and n_tok % TM_POST == 0

    h = _ln_call(x.reshape(n_tok, d), ln0_g, ln0_b)
    for l in range(depth):
        w = _layer_weights(l, w_in, conv_qkv, conv_mix, a_log, dt_bias, gdn_norm, w_mem_kv, w_br,
                           w_o, ln1_g, ln1_b, w_router, b_router, w_gate_up, b_gate_up, w_down,
                           b_down, ln2_g, ln2_b)
        km, vm = _memkv_call(mem, w["wkv"])
        q, k, v, gcol, grow, zs, part, g0 = _pre_call(h, km, vm, w, seq_len)
        o = _gdn_call(q, k, v, gcol, grow, batch, seq_len)
        h1, h1t, top_idx, top_w = _post_call(o, zs, g0, part, h, w, alpha)
        slot_tok, slot_dst, block_e = _route(top_idx, n_blocks)
        ys = _moe_call(h1t, slot_tok, slot_dst, block_e, w)
        h = _combine_call(ys, top_w.T, h1, w["ln2_g"], w["ln2_b"], alpha)
    return h.reshape(batch, seq_len, d)
```

```python
import functools

import jax
import jax.numpy as jnp
from jax import lax
from jax.experimental import pallas as pl
from jax.experimental.pallas import tpu as pltpu

F32 = jnp.float32
BF16 = jnp.bfloat16
HIGHEST = lax.Precision.HIGHEST

D_MODEL = 1024
N_HEADS = 4
HEAD_DIM = 128
WIDTH = N_HEADS * HEAD_DIM
GDN_CONV = 4
SHORT_CONV = 3
N_EXPERTS = 32
TOP_K = 4
D_FF = 1024
SWIGLU_LIMIT = 7.0
SWIGLU_ALPHA = 1.702
MOE_BLOCK = 256
N_ROWBUF = 3
LN_EPS = 1e-5
RMS_EPS = 1e-6
HALO = 8
LANES = 128
VMEM_LIMIT = 56 * 1024 * 1024

TM_PRE = 512
TM_POST = 512
TM_LN = 512
TT_GDN = 256
GDN_CHUNK = 64


def _dot(a, b, precision=None):
    return jnp.dot(a, b, preferred_element_type=F32, precision=precision)


def _dot_nt(a, b, precision=None):
    return lax.dot_general(a, b, (((1,), (1,)), ((), ())), preferred_element_type=F32,
                           precision=precision)


def _dot_tn(a, b):
    return lax.dot_general(a, b, (((0,), (0,)), ((), ())), preferred_element_type=F32)


def _layer_norm(y, g, b):
    mu = jnp.mean(y, axis=-1, keepdims=True)
    d = y - mu
    var = jnp.mean(d * d, axis=-1, keepdims=True)
    return d * lax.rsqrt(var + LN_EPS) * g + b


def _sigmoid(x):
    return 0.5 * jnp.tanh(0.5 * x) + 0.5


def _silu(x):
    return x * _sigmoid(x)


def _softplus(x):
    return jnp.maximum(x, 0.0) + jnp.log(1.0 + jnp.exp(-jnp.abs(x)))


def _const_spec(shape):
    zeros = (0,) * len(shape)
    return pl.BlockSpec(shape, lambda *_: zeros, pipeline_mode=pl.Buffered(1))


def _params(*sem):
    return pltpu.CompilerParams(dimension_semantics=sem, vmem_limit_bytes=VMEM_LIMIT)


def _ln_body(x_ref, g_ref, b_ref, o_ref):
    o_ref[...] = _layer_norm(x_ref[...], g_ref[...], b_ref[...])


def _ln_call(x, g, b):
    n, d = x.shape
    row = pl.BlockSpec((TM_LN, d), lambda i: (i, 0))
    return pl.pallas_call(
        _ln_body, out_shape=jax.ShapeDtypeStruct((n, d), F32), grid=(n // TM_LN,),
        in_specs=[row, _const_spec((1, d)), _const_spec((1, d))], out_specs=row,
        compiler_params=_params("parallel"), name="ln0")(x, g.reshape(1, d), b.reshape(1, d))


def _memkv_body(mem_ref, w_ref, k_ref, v_ref):
    kv = _dot(mem_ref[0].astype(BF16), w_ref[...])
    k_ref[0] = kv[:, :WIDTH].astype(BF16)
    v_ref[0] = kv[:, WIDTH:].astype(BF16)


def _memkv_call(mem, w_kv):
    b, m, d = mem.shape
    out = jax.ShapeDtypeStruct((b, m, WIDTH), BF16)
    blk = pl.BlockSpec((1, m, WIDTH), lambda i: (i, 0, 0))
    return pl.pallas_call(
        _memkv_body, out_shape=(out, out), grid=(b,),
        in_specs=[pl.BlockSpec((1, m, d), lambda i: (i, 0, 0)), _const_spec((d, 2 * WIDTH))],
        out_specs=(blk, blk), compiler_params=_params("parallel"), name="memkv")(mem, w_kv)


def _causal_conv(x, halo_ref, buf_ref, w_ref, first_tile):
    tm = x.shape[0]
    taps = w_ref.shape[0]

    @pl.when(first_tile)
    def _():
        halo_ref[...] = jnp.zeros_like(halo_ref)

    buf_ref[0:HALO, :] = halo_ref[...]
    buf_ref[HALO:HALO + tm, :] = x
    halo_ref[...] = x[tm - HALO:, :]
    acc = w_ref[taps - 1:taps, :] * x
    for j in range(taps - 1):
        off = HALO - (taps - 1) + j
        acc = acc + w_ref[j:j + 1, :] * buf_ref[off:off + tm, :]
    return acc


def _pre_body(h_ref, wq_ref, wk_ref, wv_ref, wab_ref, wabt_ref, wz_ref, wxc_ref, wgb_ref,
              wgc_ref, wqm_ref, wgl_ref, cq_ref, ck_ref, cv_ref, cmix_ref, abp_ref, abpt_ref,
              km_ref, vm_ref, wb1_ref, wb2_ref,
              q_ref, k_ref, v_ref, gcol_ref, grow_ref, zs_ref, part_ref, g0_ref,
              hq_ref, hk_ref, hv_ref, hx_ref, bufq_ref, bufk_ref, bufv_ref, bufx_ref, *,
              tiles_per_seq):
    first = pl.program_id(0) % tiles_per_seq == 0
    hb = h_ref[...].astype(BF16)

    def head_l2(x, scale):
        outs = []
        for hd in range(N_HEADS):
            xh = x[:, hd * HEAD_DIM:(hd + 1) * HEAD_DIM]
            ss = jnp.sum(xh * xh, axis=-1, keepdims=True)
            outs.append(xh * (lax.rsqrt(ss + RMS_EPS) * scale))
        return jnp.concatenate(outs, axis=-1)

    proj_q = _dot(hb, wq_ref[...])
    proj_k = _dot(hb, wk_ref[...])
    q = _silu(_causal_conv(proj_q, hq_ref, bufq_ref, cq_ref, first))
    q_ref[...] = head_l2(q, HEAD_DIM ** -0.5)
    proj_v = _dot(hb, wv_ref[...])
    k = _silu(_causal_conv(proj_k, hk_ref, bufk_ref, ck_ref, first))
    k_ref[...] = head_l2(k, 1.0)
    proj_z = _dot(hb, wz_ref[...])
    v_ref[...] = _silu(_causal_conv(proj_v, hv_ref, bufv_ref, cv_ref, first))
    xc = _dot(hb, wxc_ref[...])
    gc = _dot(hb, wgc_ref[...])
    zs_ref[...] = _silu(proj_z)
    proj_gb = _dot(hb, wgb_ref[...])

    ab = _dot(hb, wab_ref[...])
    a_scale = abp_ref[0:1, :]
    a_bias = abp_ref[1:2, :]
    lane = lax.broadcasted_iota(jnp.int32, ab.shape, 1)
    gb_col = jnp.where(lane < N_HEADS, a_scale * _softplus(ab + a_bias), _sigmoid(ab))
    gcol_ref[...] = gb_col[:, :2 * N_HEADS]
    abt = _dot_nt(wabt_ref[...], hb)
    sub = lax.broadcasted_iota(jnp.int32, abt.shape, 0)
    grow_ref[...] = jnp.where(sub < N_HEADS,
                              abpt_ref[:, 0:1] * _softplus(abt + abpt_ref[:, 1:2]), _sigmoid(abt))

    qm = _dot(hb, wqm_ref[...]).astype(BF16)
    y_b = proj_gb * _causal_conv(gc * xc, hx_ref, bufx_ref, cmix_ref, first)
    gl0 = _dot(hb, wgl_ref[:, 0:D_MODEL])
    p_b = _dot(y_b.astype(BF16), wb1_ref[...])

    heads = []
    for hd in range(N_HEADS):
        sl = slice(hd * HEAD_DIM, (hd + 1) * HEAD_DIM)
        s = _dot_nt(qm[:, sl], km_ref[0, :, sl]) * (HEAD_DIM ** -0.5)
        e = jnp.exp(s - jnp.max(s, axis=-1, keepdims=True))
        p = e / jnp.sum(e, axis=-1, keepdims=True)
        heads.append(_dot(p.astype(BF16), vm_ref[0, :, sl]))
    y_m = jnp.concatenate(heads, axis=-1)
    gl1 = _dot(hb, wgl_ref[:, D_MODEL:2 * D_MODEL])
    g0_ref[...] = _sigmoid(gl0)

    p_m = _dot(y_m.astype(BF16), wb2_ref[...])
    gl2 = _dot(hb, wgl_ref[:, 2 * D_MODEL:3 * D_MODEL])
    part_ref[...] = _sigmoid(gl1) * p_b + _sigmoid(gl2) * p_m


def _pre_call(h, km, vm, w, seq_len):
    n, d = h.shape
    tm = TM_PRE
    tiles_per_seq = seq_len // tm
    row = lambda c: pl.BlockSpec((tm, c), lambda i: (i, 0))
    mem_spec = pl.BlockSpec((1,) + km.shape[1:], lambda i: (i // tiles_per_seq, 0, 0))
    consts = [w["wq"], w["wk"], w["wv"], w["wab"], w["wabt"], w["wz"], w["wxc"], w["wgb"],
              w["wgc"], w["wqm"], w["wgl"], w["cq"], w["ck"], w["cv"], w["cmix"], w["abp"],
              w["abpt"]]
    tail = [w["wb1"], w["wb2"]]
    out_shape = (
        jax.ShapeDtypeStruct((n, WIDTH), F32),
        jax.ShapeDtypeStruct((n, WIDTH), F32),
        jax.ShapeDtypeStruct((n, WIDTH), F32),
        jax.ShapeDtypeStruct((n, 2 * N_HEADS), F32),
        jax.ShapeDtypeStruct((2 * N_HEADS, n), F32),
        jax.ShapeDtypeStruct((n, WIDTH), F32),
        jax.ShapeDtypeStruct((n, d), F32),
        jax.ShapeDtypeStruct((n, d), F32),
    )
    out_specs = (row(WIDTH), row(WIDTH), row(WIDTH), row(2 * N_HEADS),
                 pl.BlockSpec((2 * N_HEADS, tm), lambda i: (0, i)), row(WIDTH), row(d), row(d))
    scratch = [pltpu.VMEM((HALO, WIDTH), F32)] * 4 + [pltpu.VMEM((HALO + tm, WIDTH), F32)] * 4
    return pl.pallas_call(
        functools.partial(_pre_body, tiles_per_seq=tiles_per_seq),
        out_shape=out_shape, grid=(n // tm,),
        in_specs=[row(d)] + [_const_spec(c.shape) for c in consts] + [mem_spec, mem_spec]
        + [_const_spec(c.shape) for c in tail],
        out_specs=out_specs, scratch_shapes=scratch,
        compiler_params=_params("arbitrary"), name="mixer_pre")(h, *consts, km, vm, *tail)


def _gdn_body(q_ref, k_ref, v_ref, gcol_ref, grow_ref, o_ref, s_ref, *, chunk):
    @pl.when(pl.program_id(1) == 0)
    def _():
        s_ref[...] = jnp.zeros_like(s_ref)

    tt = q_ref.shape[0]
    n_chunks = tt // chunk
    shift = chunk.bit_length() - 1
    row = lax.broadcasted_iota(jnp.int32, (tt, tt), 0)
    col = lax.broadcasted_iota(jnp.int32, (tt, tt), 1)
    same = (row >> shift) == (col >> shift)
    causal = jnp.logical_and(same, row >= col)
    strict = jnp.logical_and(same, row > col)
    eye = (row == col).astype(F32)
    causal_f = causal.astype(F32)
    same_f = same.astype(F32)
    n_doublings = chunk.bit_length() - 2

    gcol = gcol_ref[...]
    grow = grow_ref[...]
    gcum_col = _dot(causal_f, gcol, precision=HIGHEST)
    gcum_row = _dot_nt(grow, causal_f, precision=HIGHEST)
    gtot_col = _dot(same_f, gcol, precision=HIGHEST)
    gtot_row = _dot(grow, same_f, precision=HIGHEST)

    heads = []
    for hd in range(N_HEADS):
        lanes = slice(hd * HEAD_DIM, (hd + 1) * HEAD_DIM)
        k = k_ref[:, lanes]
        gc = gcum_col[:, hd:hd + 1]
        gr = gcum_row[hd:hd + 1, :]
        beta = gcol[:, N_HEADS + hd:N_HEADS + hd + 1]
        decay = jnp.exp(jnp.where(causal, gc - gr, -1e30))
        kb = k * beta
        k16 = k.astype(BF16)
        low = jnp.where(strict, _dot_nt(kb.astype(BF16), k16) * decay, 0.0)
        heads.append(dict(gc=gc, beta=beta, decay=decay, e_gc=jnp.exp(gc), k=k, kb=kb, k16=k16,
                          inv=eye - low, power=low))
    for _ in range(n_doublings):
        for head in heads:
            p16 = head["power"].astype(BF16)
            head["power"] = _dot(p16, p16)
        for head in heads:
            head["inv"] = head["inv"] + _dot(head["inv"].astype(BF16), head["power"].astype(BF16))
    for hd, head in enumerate(heads):
        lanes = slice(hd * HEAD_DIM, (hd + 1) * HEAD_DIM)
        q = q_ref[:, lanes]
        rhs = jnp.concatenate([v_ref[:, lanes] * head["beta"], head["kb"] * head["e_gc"]],
                              axis=-1).astype(BF16)
        uw = _dot(head["inv"].astype(BF16), rhs)
        head.update(u=uw[:, :HEAD_DIM], w=uw[:, HEAD_DIM:], qd=q * head["e_gc"],
                    ke=head["k"] * jnp.exp(gtot_col[:, hd:hd + 1] - head["gc"]),
                    qk=(_dot_nt(q.astype(BF16), head["k16"]) * head["decay"]).astype(BF16))

    pair = 2 * HEAD_DIM
    n_pairs = N_HEADS // 2
    prow = lax.broadcasted_iota(jnp.int32, (pair, pair), 0)
    pcol = lax.broadcasted_iota(jnp.int32, (pair, pair), 1)
    block_diag = (prow >= HEAD_DIM) == (pcol >= HEAD_DIM)
    first_head = lax.broadcasted_iota(jnp.int32, (1, pair), 1) < HEAD_DIM
    two = lambda p, name: jnp.concatenate([heads[2 * p][name], heads[2 * p + 1][name]], axis=-1)
    u2, w2, qd2, ke2 = ([two(p, name) for p in range(n_pairs)] for name in ("u", "w", "qd", "ke"))
    state = [s_ref[p] for p in range(n_pairs)]
    o_state = [[] for _ in range(n_pairs)]
    v_new = [[] for _ in range(n_pairs)]
    for ci in range(n_chunks):
        rows = slice(ci * chunk, (ci + 1) * chunk)
        at = slice(ci * chunk, ci * chunk + 1)
        for p in range(n_pairs):
            lhs = jnp.concatenate([w2[p][rows], qd2[p][rows]], axis=0).astype(BF16)
            res = _dot(lhs, state[p].astype(BF16))
            vn = u2[p][rows] - res[:chunk]
            o_state[p].append(res[chunk:])
            v_new[p].append(vn)
            chunk_decay = jnp.exp(jnp.where(first_head, gtot_row[2 * p:2 * p + 1, at],
                                            gtot_row[2 * p + 1:2 * p + 2, at]))
            outer = _dot_tn(ke2[p][rows].astype(BF16), vn.astype(BF16))
            state[p] = state[p] * chunk_decay + jnp.where(block_diag, outer, 0.0)
    for p in range(n_pairs):
        s_ref[p] = state[p]
        o_pair = jnp.concatenate(o_state[p], axis=0)
        v_pair = jnp.concatenate(v_new[p], axis=0).astype(BF16)
        for j in range(2):
            lanes = slice(j * HEAD_DIM, (j + 1) * HEAD_DIM)
            out_lanes = slice((2 * p + j) * HEAD_DIM, (2 * p + j + 1) * HEAD_DIM)
            o_ref[:, out_lanes] = o_pair[:, lanes] + _dot(heads[2 * p + j]["qk"], v_pair[:, lanes])


def _gdn_call(q, k, v, gcol, grow, batch, seq_len):
    n = q.shape[0]
    tt = TT_GDN
    steps = seq_len // tt
    tok = pl.BlockSpec((tt, WIDTH), lambda b, t: (b * steps + t, 0))
    return pl.pallas_call(
        functools.partial(_gdn_body, chunk=GDN_CHUNK),
        out_shape=jax.ShapeDtypeStruct((n, WIDTH), F32), grid=(batch, steps),
        in_specs=[tok, tok, tok,
                  pl.BlockSpec((tt, 2 * N_HEADS), lambda b, t: (b * steps + t, 0)),
                  pl.BlockSpec((2 * N_HEADS, tt), lambda b, t: (0, b * steps + t))],
        out_specs=tok,
        scratch_shapes=[pltpu.VMEM((N_HEADS // 2, 2 * HEAD_DIM, 2 * HEAD_DIM), F32)],
        compiler_params=_params("parallel", "arbitrary"), name="gdn")(q, k, v, gcol, grow)


def _post_body(o_ref, zs_ref, g0_ref, part_ref, h_ref, gn_ref, wb0_ref, wo_ref, lg_ref, lb_ref,
               wr_ref, br_ref, h1_ref, h1t_ref, idx_ref, tw_ref, *, alpha):
    o = o_ref[...]
    zs = zs_ref[...]
    heads = []
    for hd in range(N_HEADS):
        sl = slice(hd * HEAD_DIM, (hd + 1) * HEAD_DIM)
        oh = o[:, sl]
        ms = jnp.mean(oh * oh, axis=-1, keepdims=True)
        heads.append(oh * lax.rsqrt(ms + RMS_EPS) * gn_ref[...] * zs[:, sl])
    y_a = jnp.concatenate(heads, axis=-1).astype(BF16)
    merged = g0_ref[...] * _dot(y_a, wb0_ref[...]) + part_ref[...]
    mix = _dot(merged.astype(BF16), wo_ref[...])
    h1 = _layer_norm(alpha * h_ref[...] + mix, lg_ref[...], lb_ref[...])
    h1_ref[...] = h1
    h1t_ref[...] = h1.reshape(h1.shape[0], D_MODEL // LANES, LANES)

    logits = _dot_nt(wr_ref[...], h1, precision=HIGHEST) + br_ref[...]
    expert = lax.broadcasted_iota(jnp.int32, logits.shape, 0)
    idx, top = [], []
    for _ in range(TOP_K):
        m = jnp.max(logits, axis=0, keepdims=True)
        sel = jnp.min(jnp.where(logits == m, expert, N_EXPERTS), axis=0, keepdims=True)
        idx.append(sel)
        top.append(m)
        logits = jnp.where(expert == sel, -jnp.inf, logits)
    e = jnp.exp(jnp.concatenate(top, axis=0) - top[0])
    idx_ref[...] = jnp.concatenate(idx, axis=0)
    tw_ref[...] = e / jnp.sum(e, axis=0, keepdims=True)


def _post_call(o, zs, g0, part, h, w, alpha):
    n, d = h.shape
    tm = TM_POST
    row = lambda c: pl.BlockSpec((tm, c), lambda i: (i, 0))
    consts = [w["gn"], w["wb0"], w["wo"], w["ln1_g"], w["ln1_b"], w["wr"], w["br"]]
    return pl.pallas_call(
        functools.partial(_post_body, alpha=alpha),
        out_shape=(jax.ShapeDtypeStruct((n, d), F32),
                   jax.ShapeDtypeStruct((n, d // LANES, LANES), F32),
                   jax.ShapeDtypeStruct((TOP_K, n), jnp.int32),
                   jax.ShapeDtypeStruct((TOP_K, n), F32)),
        grid=(n // tm,),
        in_specs=[row(WIDTH), row(WIDTH), row(d), row(d), row(d)]
        + [_const_spec(c.shape) for c in consts],
        out_specs=(row(d), pl.BlockSpec((tm, d // LANES, LANES), lambda i: (i, 0, 0)),
                   pl.BlockSpec((TOP_K, tm), lambda i: (0, i)),
                   pl.BlockSpec((TOP_K, tm), lambda i: (0, i))),
        compiler_params=_params("parallel"), name="mixer_post")(o, zs, g0, part, h, *consts)


def _route(top_idx, n_blocks):
    n_tok = top_idx.shape[1]
    n_assign = n_tok * TOP_K
    n_slots = n_blocks * MOE_BLOCK
    flat_e = top_idx.T.reshape(n_assign)
    seg = 256
    onehot = (flat_e[:, None] == jnp.arange(N_EXPERTS, dtype=jnp.int32)[None, :]).astype(F32)
    oh3 = onehot.reshape(n_assign // seg, seg, N_EXPERTS)
    within = jnp.einsum("ij,bje->bie", jnp.tril(jnp.ones((seg, seg), F32)), oh3)
    seg_tot = within[:, -1, :]
    seg_off = jnp.cumsum(seg_tot, axis=0) - seg_tot
    rank = (jnp.sum(oh3 * (within + seg_off[:, None, :]), axis=-1) - 1.0).astype(jnp.int32)
    rank = rank.reshape(n_assign)
    counts = (seg_off[-1] + seg_tot[-1]).astype(jnp.int32)
    padded = (counts + MOE_BLOCK - 1) // MOE_BLOCK * MOE_BLOCK
    pad_end = jnp.cumsum(padded)
    dest = (pad_end - padded)[flat_e] + rank
    slot_a = jnp.full((n_slots,), -1, jnp.int32).at[dest].set(jnp.arange(n_assign, dtype=jnp.int32))
    is_pad = slot_a < 0
    pad_rank = jnp.cumsum(is_pad.astype(jnp.int32)) - 1
    a = jnp.maximum(slot_a, 0)
    slot_tok = a // TOP_K
    slot_dst = jnp.where(is_pad, n_assign + pad_rank, (a % TOP_K) * n_tok + slot_tok)
    block_start = jnp.arange(n_blocks + 1, dtype=jnp.int32) * MOE_BLOCK
    block_e = jnp.minimum(jnp.sum((pad_end[None, :] <= block_start[:, None]).astype(jnp.int32),
                                  axis=1), N_EXPERTS - 1)
    filler_tok = jnp.zeros((3 * MOE_BLOCK,), jnp.int32)
    filler_dst = n_slots + jnp.arange(MOE_BLOCK, dtype=jnp.int32)
    slot_tok = jnp.concatenate([slot_tok.astype(jnp.int32), filler_tok])
    slot_dst = jnp.concatenate([filler_dst, slot_dst.astype(jnp.int32)])
    return (slot_tok.reshape(n_blocks + 3, 1, MOE_BLOCK),
            slot_dst.reshape(n_blocks + 1, 1, MOE_BLOCK), block_e)


def _moe_body(be_ref, tok0_ref, tok1_ref, tok_ref, dst_ref, h_hbm, wgu_ref, bgu_ref, wd_ref,
              bd_ref, ys_hbm, xbuf, ybuf, wgu16, wd16, gsem, ssem, *, n_blocks):
    i = pl.program_id(0)
    cur = i % N_ROWBUF
    nxt = (i + 2) % N_ROWBUF

    def gather_row(table_ref, r, slot):
        return pltpu.make_async_copy(h_hbm.at[table_ref[0, 0, r]], xbuf.at[slot, r],
                                     gsem.at[slot])

    def gather_wait(slot):
        pltpu.make_async_copy(h_hbm.at[pl.ds(0, MOE_BLOCK)], xbuf.at[slot],
                              gsem.at[slot]).wait()

    def scatter_wait(slot):
        pltpu.make_async_copy(ybuf.at[slot], ys_hbm.at[pl.ds(0, MOE_BLOCK)],
                              ssem.at[slot]).wait()

    @pl.when(i == 0)
    def _():
        ybuf[...] = jnp.zeros_like(ybuf)
        for r in range(MOE_BLOCK):
            gather_row(tok0_ref, r, 0).start()
        for r in range(MOE_BLOCK):
            gather_row(tok1_ref, r, 1).start()

    @pl.when(jnp.logical_or(i == 0, be_ref[i] != be_ref[jnp.maximum(i - 1, 0)]))
    def _():
        wgu16[...] = wgu_ref[0, 0].astype(BF16)
        wd16[...] = wd_ref[0, 0].astype(BF16)

    gather_wait(cur)

    @pl.when(i >= 2)
    def _():
        scatter_wait(cur)

    x16 = xbuf[cur].reshape(MOE_BLOCK, D_MODEL).astype(BF16)
    for r in range(MOE_BLOCK):
        pltpu.make_async_copy(ybuf.at[nxt, r], ys_hbm.at[dst_ref[0, 0, r]], ssem.at[nxt]).start()
    for r in range(MOE_BLOCK):
        gather_row(tok_ref, r, nxt).start()
    gu = _dot(x16, wgu16[...]) + bgu_ref[0, 0]
    gate = jnp.minimum(gu[:, :D_FF], SWIGLU_LIMIT)
    up = jnp.clip(gu[:, D_FF:], -SWIGLU_LIMIT, SWIGLU_LIMIT)
    act = (up + 1.0) * (gate * _sigmoid(gate * SWIGLU_ALPHA))
    y = _dot(act.astype(BF16), wd16[...]) + bd_ref[0, 0]
    ybuf[cur] = y.reshape(MOE_BLOCK, D_MODEL // LANES, LANES)

    @pl.when(i == n_blocks)
    def _():
        gather_wait((i + 1) % N_ROWBUF)
        gather_wait(nxt)
        scatter_wait((i + 1) % N_ROWBUF)
        scatter_wait(nxt)


def _moe_call(h1t, slot_tok, slot_dst, block_e, layer, w_gate_up, b_gate_up, w_down, b_down):
    n, sub, lanes = h1t.shape
    d = sub * lanes
    n_blocks = block_e.shape[0] - 1
    table = lambda f: pl.BlockSpec((1, 1, MOE_BLOCK), f, memory_space=pltpu.SMEM)
    per_expert = lambda r, c: pl.BlockSpec((1, 1, r, c), lambda i, be: (layer, be[i], 0, 0))
    shape4 = lambda b: b.reshape(b.shape[0], N_EXPERTS, 1, b.shape[-1])
    grid_spec = pltpu.PrefetchScalarGridSpec(
        num_scalar_prefetch=1, grid=(n_blocks + 1,),
        in_specs=[table(lambda i, be: (0, 0, 0)), table(lambda i, be: (1, 0, 0)),
                  table(lambda i, be: (i + 2, 0, 0)), table(lambda i, be: (i, 0, 0)),
                  pl.BlockSpec(memory_space=pl.ANY),
                  per_expert(d, 2 * D_FF), per_expert(1, 2 * D_FF),
                  per_expert(D_FF, d), per_expert(1, d)],
        out_specs=pl.BlockSpec(memory_space=pl.ANY),
        scratch_shapes=[pltpu.VMEM((N_ROWBUF, MOE_BLOCK, sub, lanes), F32),
                        pltpu.VMEM((N_ROWBUF, MOE_BLOCK, sub, lanes), F32),
                        pltpu.VMEM((d, 2 * D_FF), BF16), pltpu.VMEM((D_FF, d), BF16),
                        pltpu.SemaphoreType.DMA((N_ROWBUF,)),
                        pltpu.SemaphoreType.DMA((N_ROWBUF,))])
    return pl.pallas_call(
        functools.partial(_moe_body, n_blocks=n_blocks),
        out_shape=jax.ShapeDtypeStruct(((n_blocks + 1) * MOE_BLOCK, sub, lanes), F32),
        grid_spec=grid_spec, compiler_params=_params("arbitrary"), name="moe")(
            block_e, slot_tok, slot_tok, slot_tok, slot_dst, h1t,
            w_gate_up, shape4(b_gate_up), w_down, shape4(b_down))


def _combine_body(y0_ref, y1_ref, y2_ref, y3_ref, tw_ref, h_ref, g_ref, b_ref, o_ref, *, alpha):
    tw = tw_ref[...]
    ffn = None
    for j, y_ref in enumerate((y0_ref, y1_ref, y2_ref, y3_ref)):
        term = tw[:, j:j + 1] * y_ref[...].reshape(h_ref.shape)
        ffn = term if ffn is None else ffn + term
    o_ref[...] = _layer_norm(alpha * h_ref[...] + ffn, g_ref[...], b_ref[...])


def _combine_call(ys, top_w, h1, g, b, alpha):
    n, d = h1.shape
    tm = TM_POST
    tiles = n // tm
    row = pl.BlockSpec((tm, d), lambda i: (i, 0))
    y_specs = [pl.BlockSpec((tm,) + ys.shape[1:], lambda i, j=j: (j * tiles + i, 0, 0))
               for j in range(TOP_K)]
    return pl.pallas_call(
        functools.partial(_combine_body, alpha=alpha),
        out_shape=jax.ShapeDtypeStruct((n, d), F32), grid=(tiles,),
        in_specs=y_specs + [pl.BlockSpec((tm, TOP_K), lambda i: (i, 0)), row,
                            _const_spec((1, d)), _const_spec((1, d))],
        out_specs=row, compiler_params=_params("parallel"), name="combine")(
            ys, ys, ys, ys, top_w, h1, g, b)


def _layer_weights(l, w_in, conv_qkv, conv_mix, a_log, dt_bias, gdn_norm, w_mem_kv, w_br, w_o,
                   ln1_g, ln1_b, w_router, b_router, ln2_g, ln2_b):
    d = D_MODEL
    wi = w_in[l].astype(BF16)
    c0 = 3 * WIDTH
    c_z = c0 + 2 * N_HEADS
    c_x = c_z + WIDTH
    c_gl = c_x + 4 * WIDTH
    wab = jnp.zeros((d, LANES), BF16).at[:, :2 * N_HEADS].set(wi[:, c0:c_z])
    neg_a = -jnp.exp(a_log[l].astype(F32))
    dtb = dt_bias[l].astype(F32)
    abp = jnp.zeros((2, LANES), F32).at[0, :N_HEADS].set(neg_a).at[1, :N_HEADS].set(dtb)
    abpt = jnp.zeros((2 * N_HEADS, 2), F32).at[:N_HEADS, 0].set(neg_a).at[:N_HEADS, 1].set(dtb)
    cq = conv_qkv[l].astype(F32)
    return dict(
        wq=wi[:, 0:WIDTH], wk=wi[:, WIDTH:2 * WIDTH], wv=wi[:, 2 * WIDTH:c0],
        wab=wab, wabt=wi[:, c0:c_z].T, wz=wi[:, c_z:c_x],
        wxc=wi[:, c_x:c_x + WIDTH], wgb=wi[:, c_x + WIDTH:c_x + 2 * WIDTH],
        wgc=wi[:, c_x + 2 * WIDTH:c_x + 3 * WIDTH], wqm=wi[:, c_x + 3 * WIDTH:c_gl],
        wgl=wi[:, c_gl:],
        cq=cq[:, 0:WIDTH], ck=cq[:, WIDTH:2 * WIDTH], cv=cq[:, 2 * WIDTH:],
        cmix=conv_mix[l].astype(F32), abp=abp, abpt=abpt,
        wkv=w_mem_kv[l].astype(BF16),
        wb0=w_br[l, 0].astype(BF16), wb1=w_br[l, 1].astype(BF16), wb2=w_br[l, 2].astype(BF16),
        wo=w_o[l].astype(BF16), gn=gdn_norm[l].astype(F32).reshape(1, HEAD_DIM),
        ln1_g=ln1_g[l].reshape(1, d), ln1_b=ln1_b[l].reshape(1, d),
        wr=w_router[l].astype(F32).T, br=b_router[l].astype(F32).reshape(N_EXPERTS, 1),
        ln2_g=ln2_g[l].reshape(1, d), ln2_b=ln2_b[l].reshape(1, d),
    )


def kernel(x, mem, ln0_g, ln0_b, w_in, conv_qkv, conv_mix, a_log, dt_bias, gdn_norm, w_mem_kv, w_br, w_o, ln1_g, ln1_b, w_router, b_router, w_gate_up, b_gate_up, w_down, b_down, ln2_g, ln2_b):
    batch, seq_len, d = x.shape
    depth = w_in.shape[0]
    n_tok = batch * seq_len
    alpha = (2 * depth) ** 0.25
    n_assign = n_tok * TOP_K
    n_blocks = (n_assign + N_EXPERTS * (MOE_BLOCK - 1) + MOE_BLOCK - 1) // MOE_BLOCK
    assert d == D_MODEL and seq_len % TM_PRE == 0 and seq_len % TT_GDN == 0
    assert n_tok % TM_LN == 0 and n_tok % TM_POST == 0

    h = _ln_call(x.reshape(n_tok, d), ln0_g, ln0_b)
    for l in range(depth):
        w = _layer_weights(l, w_in, conv_qkv, conv_mix, a_log, dt_bias, gdn_norm, w_mem_kv, w_br,
                           w_o, ln1_g, ln1_b, w_router, b_router, ln2_g, ln2_b)
        km, vm = _memkv_call(mem, w["wkv"])
        q, k, v, gcol, grow, zs, part, g0 = _pre_call(h, km, vm, w, seq_len)
        o = _gdn_call(q, k, v, gcol, grow, batch, seq_len)
        h1, h1t, top_idx, top_w = _post_call(o, zs, g0, part, h, w, alpha)
        slot_tok, slot_dst, block_e = _route(top_idx, n_blocks)
        ys = _moe_call(h1t, slot_tok, slot_dst, block_e, l, w_gate_up, b_gate_up, w_down, b_down)
        h = _combine_call(ys, top_w.T, h1, w["ln2_g"], w["ln2_b"], alpha)
    return h.reshape(batch, seq_len, d)
```

```python
import functools

import jax
import jax.numpy as jnp
from jax import lax
from jax.experimental import pallas as pl
from jax.experimental.pallas import tpu as pltpu

F32 = jnp.float32
BF16 = jnp.bfloat16
HIGHEST = lax.Precision.HIGHEST

D_MODEL = 1024
N_HEADS = 4
HEAD_DIM = 128
WIDTH = N_HEADS * HEAD_DIM
GDN_CONV = 4
SHORT_CONV = 3
N_EXPERTS = 32
TOP_K = 4
D_FF = 1024
SWIGLU_LIMIT = 7.0
SWIGLU_ALPHA = 1.702
MOE_BLOCK = 256
N_ROWBUF = 3
LN_EPS = 1e-5
RMS_EPS = 1e-6
HALO = 8
LANES = 128
VMEM_LIMIT = 56 * 1024 * 1024

TM_PRE = 512
TM_POST = 512
TM_LN = 512
TT_GDN = 256
GDN_CHUNK = 64


def _dot(a, b, precision=None):
    return jnp.dot(a, b, preferred_element_type=F32, precision=precision)


def _dot_nt(a, b, precision=None):
    return lax.dot_general(a, b, (((1,), (1,)), ((), ())), preferred_element_type=F32,
                           precision=precision)


def _dot_tn(a, b):
    return lax.dot_general(a, b, (((0,), (0,)), ((), ())), preferred_element_type=F32)


def _layer_norm(y, g, b):
    mu = jnp.mean(y, axis=-1, keepdims=True)
    d = y - mu
    var = jnp.mean(d * d, axis=-1, keepdims=True)
    return d * lax.rsqrt(var + LN_EPS) * g + b


def _sigmoid(x):
    return 0.5 * jnp.tanh(0.5 * x) + 0.5


def _silu(x):
    return x * _sigmoid(x)


def _softplus(x):
    return jnp.maximum(x, 0.0) + jnp.log(1.0 + jnp.exp(-jnp.abs(x)))


def _const_spec(shape):
    zeros = (0,) * len(shape)
    return pl.BlockSpec(shape, lambda *_: zeros, pipeline_mode=pl.Buffered(1))


def _params(*sem):
    return pltpu.CompilerParams(dimension_semantics=sem, vmem_limit_bytes=VMEM_LIMIT)


def _ln_body(x_ref, g_ref, b_ref, o_ref):
    o_ref[...] = _layer_norm(x_ref[...], g_ref[...], b_ref[...])


def _ln_call(x, g, b):
    n, d = x.shape
    row = pl.BlockSpec((TM_LN, d), lambda i: (i, 0))
    return pl.pallas_call(
        _ln_body, out_shape=jax.ShapeDtypeStruct((n, d), F32), grid=(n // TM_LN,),
        in_specs=[row, _const_spec((1, d)), _const_spec((1, d))], out_specs=row,
        compiler_params=_params("parallel"), name="ln0")(x, g.reshape(1, d), b.reshape(1, d))


def _memkv_body(mem_ref, w_ref, k_ref, v_ref):
    kv = _dot(mem_ref[0].astype(BF16), w_ref[...])
    k_ref[0] = kv[:, :WIDTH].astype(BF16)
    v_ref[0] = kv[:, WIDTH:].astype(BF16)


def _memkv_call(mem, w_kv):
    b, m, d = mem.shape
    out = jax.ShapeDtypeStruct((b, m, WIDTH), BF16)
    blk = pl.BlockSpec((1, m, WIDTH), lambda i: (i, 0, 0))
    return pl.pallas_call(
        _memkv_body, out_shape=(out, out), grid=(b,),
        in_specs=[pl.BlockSpec((1, m, d), lambda i: (i, 0, 0)), _const_spec((d, 2 * WIDTH))],
        out_specs=(blk, blk), compiler_params=_params("parallel"), name="memkv")(mem, w_kv)


def _causal_conv(x, halo_ref, buf_ref, w_ref, first_tile):
    tm = x.shape[0]
    taps = w_ref.shape[0]

    @pl.when(first_tile)
    def _():
        halo_ref[...] = jnp.zeros_like(halo_ref)

    buf_ref[0:HALO, :] = halo_ref[...]
    buf_ref[HALO:HALO + tm, :] = x
    halo_ref[...] = x[tm - HALO:, :]
    acc = w_ref[taps - 1:taps, :] * x
    for j in range(taps - 1):
        off = HALO - (taps - 1) + j
        acc = acc + w_ref[j:j + 1, :] * buf_ref[off:off + tm, :]
    return acc


def _pre_body(h_ref, wq_ref, wk_ref, wv_ref, wab_ref, wabt_ref, wz_ref, wxc_ref, wgb_ref,
              wgc_ref, wqm_ref, wgl_ref, cq_ref, ck_ref, cv_ref, cmix_ref, abp_ref, abpt_ref,
              km_ref, vm_ref, wb1_ref, wb2_ref,
              q_ref, k_ref, v_ref, gcol_ref, grow_ref, zs_ref, part_ref, g0_ref,
              hq_ref, hk_ref, hv_ref, hx_ref, bufq_ref, bufk_ref, bufv_ref, bufx_ref, *,
              tiles_per_seq):
    first = pl.program_id(0) % tiles_per_seq == 0
    hb = h_ref[...].astype(BF16)

    def head_l2(x, scale):
        outs = []
        for hd in range(N_HEADS):
            xh = x[:, hd * HEAD_DIM:(hd + 1) * HEAD_DIM]
            ss = jnp.sum(xh * xh, axis=-1, keepdims=True)
            outs.append(xh * (lax.rsqrt(ss + RMS_EPS) * scale))
        return jnp.concatenate(outs, axis=-1)

    proj_q = _dot(hb, wq_ref[...])
    proj_k = _dot(hb, wk_ref[...])
    q = _silu(_causal_conv(proj_q, hq_ref, bufq_ref, cq_ref, first))
    q_ref[...] = head_l2(q, HEAD_DIM ** -0.5)
    proj_v = _dot(hb, wv_ref[...])
    k = _silu(_causal_conv(proj_k, hk_ref, bufk_ref, ck_ref, first))
    k_ref[...] = head_l2(k, 1.0)
    proj_z = _dot(hb, wz_ref[...])
    v_ref[...] = _silu(_causal_conv(proj_v, hv_ref, bufv_ref, cv_ref, first))
    xc = _dot(hb, wxc_ref[...])
    gc = _dot(hb, wgc_ref[...])
    zs_ref[...] = _silu(proj_z)
    proj_gb = _dot(hb, wgb_ref[...])

    ab = _dot(hb, wab_ref[...])
    a_scale = abp_ref[0:1, :]
    a_bias = abp_ref[1:2, :]
    lane = lax.broadcasted_iota(jnp.int32, ab.shape, 1)
    gb_col = jnp.where(lane < N_HEADS, a_scale * _softplus(ab + a_bias), _sigmoid(ab))
    gcol_ref[...] = gb_col[:, :2 * N_HEADS]
    abt = _dot_nt(wabt_ref[...], hb)
    sub = lax.broadcasted_iota(jnp.int32, abt.shape, 0)
    grow_ref[...] = jnp.where(sub < N_HEADS,
                              abpt_ref[:, 0:1] * _softplus(abt + abpt_ref[:, 1:2]), _sigmoid(abt))

    qm = _dot(hb, wqm_ref[...]).astype(BF16)
    y_b = proj_gb * _causal_conv(gc * xc, hx_ref, bufx_ref, cmix_ref, first)
    gl0 = _dot(hb, wgl_ref[:, 0:D_MODEL])
    p_b = _dot(y_b.astype(BF16), wb1_ref[...])

    heads = []
    for hd in range(N_HEADS):
        sl = slice(hd * HEAD_DIM, (hd + 1) * HEAD_DIM)
        s = _dot_nt(qm[:, sl], km_ref[0, :, sl]) * (HEAD_DIM ** -0.5)
        e = jnp.exp(s - jnp.max(s, axis=-1, keepdims=True))
        p = e / jnp.sum(e, axis=-1, keepdims=True)
        heads.append(_dot(p.astype(BF16), vm_ref[0, :, sl]))
    y_m = jnp.concatenate(heads, axis=-1)
    gl1 = _dot(hb, wgl_ref[:, D_MODEL:2 * D_MODEL])
    g0_ref[...] = _sigmoid(gl0)

    p_m = _dot(y_m.astype(BF16), wb2_ref[...])
    gl2 = _dot(hb, wgl_ref[:, 2 * D_MODEL:3 * D_MODEL])
    part_ref[...] = _sigmoid(gl1) * p_b + _sigmoid(gl2) * p_m


def _pre_call(h, km, vm, w, seq_len):
    n, d = h.shape
    tm = TM_PRE
    tiles_per_seq = seq_len // tm
    row = lambda c: pl.BlockSpec((tm, c), lambda i: (i, 0))
    mem_spec = pl.BlockSpec((1,) + km.shape[1:], lambda i: (i // tiles_per_seq, 0, 0))
    consts = [w["wq"], w["wk"], w["wv"], w["wab"], w["wabt"], w["wz"], w["wxc"], w["wgb"],
              w["wgc"], w["wqm"], w["wgl"], w["cq"], w["ck"], w["cv"], w["cmix"], w["abp"],
              w["abpt"]]
    tail = [w["wb1"], w["wb2"]]
    out_shape = (
        jax.ShapeDtypeStruct((n, WIDTH), F32),
        jax.ShapeDtypeStruct((n, WIDTH), F32),
        jax.ShapeDtypeStruct((n, WIDTH), F32),
        jax.ShapeDtypeStruct((n, 2 * N_HEADS), F32),
        jax.ShapeDtypeStruct((2 * N_HEADS, n), F32),
        jax.ShapeDtypeStruct((n, WIDTH), F32),
        jax.ShapeDtypeStruct((n, d), F32),
        jax.ShapeDtypeStruct((n, d), F32),
    )
    out_specs = (row(WIDTH), row(WIDTH), row(WIDTH), row(2 * N_HEADS),
                 pl.BlockSpec((2 * N_HEADS, tm), lambda i: (0, i)), row(WIDTH), row(d), row(d))
    scratch = [pltpu.VMEM((HALO, WIDTH), F32)] * 4 + [pltpu.VMEM((HALO + tm, WIDTH), F32)] * 4
    return pl.pallas_call(
        functools.partial(_pre_body, tiles_per_seq=tiles_per_seq),
        out_shape=out_shape, grid=(n // tm,),
        in_specs=[row(d)] + [_const_spec(c.shape) for c in consts] + [mem_spec, mem_spec]
        + [_const_spec(c.shape) for c in tail],
        out_specs=out_specs, scratch_shapes=scratch,
        compiler_params=_params("arbitrary"), name="mixer_pre")(h, *consts, km, vm, *tail)


def _gdn_body(q_ref, k_ref, v_ref, gcol_ref, grow_ref, o_ref, s_ref, *, chunk):
    @pl.when(pl.program_id(1) == 0)
    def _():
        s_ref[...] = jnp.zeros_like(s_ref)

    tt = q_ref.shape[0]
    n_chunks = tt // chunk
    shift = chunk.bit_length() - 1
    row = lax.broadcasted_iota(jnp.int32, (tt, tt), 0)
    col = lax.broadcasted_iota(jnp.int32, (tt, tt), 1)
    same = (row >> shift) == (col >> shift)
    causal = jnp.logical_and(same, row >= col)
    strict = jnp.logical_and(same, row > col)
    eye = (row == col).astype(F32)
    causal_f = causal.astype(F32)
    same_f = same.astype(F32)
    n_doublings = chunk.bit_length() - 2

    gcol = gcol_ref[...]
    grow = grow_ref[...]
    gcum_col = _dot(causal_f, gcol, precision=HIGHEST)
    gcum_row = _dot_nt(grow, causal_f, precision=HIGHEST)
    gtot_col = _dot(same_f, gcol, precision=HIGHEST)
    gtot_row = _dot(grow, same_f, precision=HIGHEST)

    heads = []
    for hd in range(N_HEADS):
        lanes = slice(hd * HEAD_DIM, (hd + 1) * HEAD_DIM)
        k = k_ref[:, lanes]
        gc = gcum_col[:, hd:hd + 1]
        gr = gcum_row[hd:hd + 1, :]
        beta = gcol[:, N_HEADS + hd:N_HEADS + hd + 1]
        decay = jnp.exp(jnp.where(causal, gc - gr, -1e30))
        kb = k * beta
        k16 = k.astype(BF16)
        low = jnp.where(strict, _dot_nt(kb.astype(BF16), k16) * decay, 0.0)
        heads.append(dict(gc=gc, beta=beta, decay=decay, e_gc=jnp.exp(gc), k=k, kb=kb, k16=k16,
                          inv=eye - low, power=low))
    for _ in range(n_doublings):
        for head in heads:
            p16 = head["power"].astype(BF16)
            head["power"] = _dot(p16, p16)
        for head in heads:
            head["inv"] = head["inv"] + _dot(head["inv"].astype(BF16), head["power"].astype(BF16))
    for hd, head in enumerate(heads):
        lanes = slice(hd * HEAD_DIM, (hd + 1) * HEAD_DIM)
        q = q_ref[:, lanes]
        rhs = jnp.concatenate([v_ref[:, lanes] * head["beta"], head["kb"] * head["e_gc"]],
                              axis=-1).astype(BF16)
        uw = _dot(head["inv"].astype(BF16), rhs)
        head.update(u=uw[:, :HEAD_DIM], w=uw[:, HEAD_DIM:], qd=q * head["e_gc"],
                    ke=head["k"] * jnp.exp(gtot_col[:, hd:hd + 1] - head["gc"]),
                    qk=(_dot_nt(q.astype(BF16), head["k16"]) * head["decay"]).astype(BF16))

    pair = 2 * HEAD_DIM
    n_pairs = N_HEADS // 2
    prow = lax.broadcasted_iota(jnp.int32, (pair, pair), 0)
    pcol = lax.broadcasted_iota(jnp.int32, (pair, pair), 1)
    block_diag = (prow >= HEAD_DIM) == (pcol >= HEAD_DIM)
    first_head = lax.broadcasted_iota(jnp.int32, (1, pair), 1) < HEAD_DIM
    two = lambda p, name: jnp.concatenate([heads[2 * p][name], heads[2 * p + 1][name]], axis=-1)
    u2, w2, qd2, ke2 = ([two(p, name) for p in range(n_pairs)] for name in ("u", "w", "qd", "ke"))
    state = [s_ref[p] for p in range(n_pairs)]
    o_state = [[] for _ in range(n_pairs)]
    v_new = [[] for _ in range(n_pairs)]
    for ci in range(n_chunks):
        rows = slice(ci * chunk, (ci + 1) * chunk)
        at = slice(ci * chunk, ci * chunk + 1)
        for p in range(n_pairs):
            lhs = jnp.concatenate([w2[p][rows], qd2[p][rows]], axis=0).astype(BF16)
            res = _dot(lhs, state[p].astype(BF16))
            vn = u2[p][rows] - res[:chunk]
            o_state[p].append(res[chunk:])
            v_new[p].append(vn)
            chunk_decay = jnp.exp(jnp.where(first_head, gtot_row[2 * p:2 * p + 1, at],
                                            gtot_row[2 * p + 1:2 * p + 2, at]))
            outer = _dot_tn(ke2[p][rows].astype(BF16), vn.astype(BF16))
            state[p] = state[p] * chunk_decay + jnp.where(block_diag, outer, 0.0)
    for p in range(n_pairs):
        s_ref[p] = state[p]
        o_pair = jnp.concatenate(o_state[p], axis=0)
        v_pair = jnp.concatenate(v_new[p], axis=0).astype(BF16)
        for j in range(2):
            lanes = slice(j * HEAD_DIM, (j + 1) * HEAD_DIM)
            out_lanes = slice((2 * p + j) * HEAD_DIM, (2 * p + j + 1) * HEAD_DIM)
            o_ref[:, out_lanes] = o_pair[:, lanes] + _dot(heads[2 * p + j]["qk"], v_pair[:, lanes])


def _gdn_call(q, k, v, gcol, grow, batch, seq_len):
    n = q.shape[0]
    tt = TT_GDN
    steps = seq_len // tt
    tok = pl.BlockSpec((tt, WIDTH), lambda b, t: (b * steps + t, 0))
    return pl.pallas_call(
        functools.partial(_gdn_body, chunk=GDN_CHUNK),
        out_shape=jax.ShapeDtypeStruct((n, WIDTH), F32), grid=(batch, steps),
        in_specs=[tok, tok, tok,
                  pl.BlockSpec((tt, 2 * N_HEADS), lambda b, t: (b * steps + t, 0)),
                  pl.BlockSpec((2 * N_HEADS, tt), lambda b, t: (0, b * steps + t))],
        out_specs=tok,
        scratch_shapes=[pltpu.VMEM((N_HEADS // 2, 2 * HEAD_DIM, 2 * HEAD_DIM), F32)],
        compiler_params=_params("parallel", "arbitrary"), name="gdn")(q, k, v, gcol, grow)


def _post_body(o_ref, zs_ref, g0_ref, part_ref, h_ref, gn_ref, wb0_ref, wo_ref, lg_ref, lb_ref,
               wr_ref, br_ref, h1_ref, h1t_ref, idx_ref, tw_ref, *, alpha):
    o = o_ref[...]
    zs = zs_ref[...]
    heads = []
    for hd in range(N_HEADS):
        sl = slice(hd * HEAD_DIM, (hd + 1) * HEAD_DIM)
        oh = o[:, sl]
        ms = jnp.mean(oh * oh, axis=-1, keepdims=True)
        heads.append(oh * lax.rsqrt(ms + RMS_EPS) * gn_ref[...] * zs[:, sl])
    y_a = jnp.concatenate(heads, axis=-1).astype(BF16)
    merged = g0_ref[...] * _dot(y_a, wb0_ref[...]) + part_ref[...]
    mix = _dot(merged.astype(BF16), wo_ref[...])
    h1 = _layer_norm(alpha * h_ref[...] + mix, lg_ref[...], lb_ref[...])
    h1_ref[...] = h1
    h1t_ref[...] = h1.reshape(h1.shape[0], D_MODEL // LANES, LANES)

    logits = _dot_nt(wr_ref[...], h1, precision=HIGHEST) + br_ref[...]
    expert = lax.broadcasted_iota(jnp.int32, logits.shape, 0)
    idx, top = [], []
    for _ in range(TOP_K):
        m = jnp.max(logits, axis=0, keepdims=True)
        sel = jnp.min(jnp.where(logits == m, expert, N_EXPERTS), axis=0, keepdims=True)
        idx.append(sel)
        top.append(m)
        logits = jnp.where(expert == sel, -jnp.inf, logits)
    e = jnp.exp(jnp.concatenate(top, axis=0) - top[0])
    idx_ref[...] = jnp.concatenate(idx, axis=0)
    tw_ref[...] = e / jnp.sum(e, axis=0, keepdims=True)


def _post_call(o, zs, g0, part, h, w, alpha):
    n, d = h.shape
    tm = TM_POST
    row = lambda c: pl.BlockSpec((tm, c), lambda i: (i, 0))
    consts = [w["gn"], w["wb0"], w["wo"], w["ln1_g"], w["ln1_b"], w["wr"], w["br"]]
    return pl.pallas_call(
        functools.partial(_post_body, alpha=alpha),
        out_shape=(jax.ShapeDtypeStruct((n, d), F32),
                   jax.ShapeDtypeStruct((n, d // LANES, LANES), F32),
                   jax.ShapeDtypeStruct((TOP_K, n), jnp.int32),
                   jax.ShapeDtypeStruct((TOP_K, n), F32)),
        grid=(n // tm,),
        in_specs=[row(WIDTH), row(WIDTH), row(d), row(d), row(d)]
        + [_const_spec(c.shape) for c in consts],
        out_specs=(row(d), pl.BlockSpec((tm, d // LANES, LANES), lambda i: (i, 0, 0)),
                   pl.BlockSpec((TOP_K, tm), lambda i: (0, i)),
                   pl.BlockSpec((TOP_K, tm), lambda i: (0, i))),
        compiler_params=_params("parallel"), name="mixer_post")(o, zs, g0, part, h, *consts)


def _route(top_idx, n_blocks):
    n_tok = top_idx.shape[1]
    n_assign = n_tok * TOP_K
    n_slots = n_blocks * MOE_BLOCK
    flat_e = top_idx.T.reshape(n_assign)
    seg = 256
    onehot = (flat_e[:, None] == jnp.arange(N_EXPERTS, dtype=jnp.int32)[None, :]).astype(F32)
    oh3 = onehot.reshape(n_assign // seg, seg, N_EXPERTS)
    within = jnp.einsum("ij,bje->bie", jnp.tril(jnp.ones((seg, seg), F32)), oh3)
    seg_tot = within[:, -1, :]
    seg_off = jnp.cumsum(seg_tot, axis=0) - seg_tot
    rank = (jnp.sum(oh3 * (within + seg_off[:, None, :]), axis=-1) - 1.0).astype(jnp.int32)
    rank = rank.reshape(n_assign)
    counts = (seg_off[-1] + seg_tot[-1]).astype(jnp.int32)
    padded = (counts + MOE_BLOCK - 1) // MOE_BLOCK * MOE_BLOCK
    pad_end = jnp.cumsum(padded)
    dest = (pad_end - padded)[flat_e] + rank
    slot_a = jnp.full((n_slots,), -1, jnp.int32).at[dest].set(
        jnp.arange(n_assign, dtype=jnp.int32), unique_indices=True, mode="promise_in_bounds")
    is_pad = slot_a < 0
    pad_rank = jnp.cumsum(is_pad.astype(jnp.int32)) - 1
    a = jnp.maximum(slot_a, 0)
    slot_tok = a // TOP_K
    slot_dst = jnp.where(is_pad, n_assign + pad_rank, (a % TOP_K) * n_tok + slot_tok)
    block_start = jnp.arange(n_blocks + 1, dtype=jnp.int32) * MOE_BLOCK
    block_e = jnp.minimum(jnp.sum((pad_end[None, :] <= block_start[:, None]).astype(jnp.int32),
                                  axis=1), N_EXPERTS - 1)
    filler_tok = jnp.zeros((3 * MOE_BLOCK,), jnp.int32)
    filler_dst = n_slots + jnp.arange(MOE_BLOCK, dtype=jnp.int32)
    tok_blocks = jnp.concatenate([slot_tok.astype(jnp.int32), filler_tok]).reshape(-1, MOE_BLOCK)
    dst_blocks = jnp.concatenate([filler_dst, slot_dst.astype(jnp.int32)]).reshape(-1, MOE_BLOCK)
    steps = n_blocks + 1
    table = jnp.stack([tok_blocks[2:], dst_blocks,
                       jnp.broadcast_to(tok_blocks[0], (steps, MOE_BLOCK)),
                       jnp.broadcast_to(tok_blocks[1], (steps, MOE_BLOCK))], axis=1)
    return table, block_e


TAB_GATHER, TAB_SCATTER, TAB_FIRST, TAB_SECOND = range(4)


def _moe_body(be_ref, tab_ref, h_hbm, wgu_ref, bgu_ref, wd_ref,
              bd_ref, ys_hbm, xbuf, ybuf, wgu16, wd16, gsem, ssem, *, n_blocks):
    i = pl.program_id(0)
    cur = i % N_ROWBUF
    nxt = (i + 2) % N_ROWBUF

    def gather_row(table_row, r, slot):
        return pltpu.make_async_copy(h_hbm.at[tab_ref[0, table_row, r]], xbuf.at[slot, r],
                                     gsem.at[slot])

    def gather_wait(slot):
        pltpu.make_async_copy(h_hbm.at[pl.ds(0, MOE_BLOCK)], xbuf.at[slot],
                              gsem.at[slot]).wait()

    def scatter_wait(slot):
        pltpu.make_async_copy(ybuf.at[slot], ys_hbm.at[pl.ds(0, MOE_BLOCK)],
                              ssem.at[slot]).wait()

    @pl.when(i == 0)
    def _():
        ybuf[...] = jnp.zeros_like(ybuf)
        for r in range(MOE_BLOCK):
            gather_row(TAB_FIRST, r, 0).start()
        for r in range(MOE_BLOCK):
            gather_row(TAB_SECOND, r, 1).start()

    @pl.when(jnp.logical_or(i == 0, be_ref[i] != be_ref[jnp.maximum(i - 1, 0)]))
    def _():
        wgu16[...] = wgu_ref[0, 0].astype(BF16)
        wd16[...] = wd_ref[0, 0].astype(BF16)

    gather_wait(cur)

    @pl.when(i >= 2)
    def _():
        scatter_wait(cur)

    x16 = xbuf[cur].reshape(MOE_BLOCK, D_MODEL).astype(BF16)
    for r in range(MOE_BLOCK):
        pltpu.make_async_copy(ybuf.at[nxt, r], ys_hbm.at[tab_ref[0, TAB_SCATTER, r]], ssem.at[nxt]).start()
    for r in range(MOE_BLOCK):
        gather_row(TAB_GATHER, r, nxt).start()
    gu = _dot(x16, wgu16[...]) + bgu_ref[0, 0]
    gate = jnp.minimum(gu[:, :D_FF], SWIGLU_LIMIT)
    up = jnp.clip(gu[:, D_FF:], -SWIGLU_LIMIT, SWIGLU_LIMIT)
    act = (up + 1.0) * (gate * _sigmoid(gate * SWIGLU_ALPHA))
    y = _dot(act.astype(BF16), wd16[...]) + bd_ref[0, 0]
    ybuf[cur] = y.reshape(MOE_BLOCK, D_MODEL // LANES, LANES)

    @pl.when(i == n_blocks)
    def _():
        gather_wait((i + 1) % N_ROWBUF)
        gather_wait(nxt)
        scatter_wait((i + 1) % N_ROWBUF)
        scatter_wait(nxt)


def _moe_call(h1t, table, block_e, layer, w_gate_up, b_gate_up, w_down, b_down):
    n, sub, lanes = h1t.shape
    d = sub * lanes
    n_blocks = block_e.shape[0] - 1
    per_expert = lambda r, c: pl.BlockSpec((1, 1, r, c), lambda i, be: (layer, be[i], 0, 0))
    shape4 = lambda b: b.reshape(b.shape[0], N_EXPERTS, 1, b.shape[-1])
    grid_spec = pltpu.PrefetchScalarGridSpec(
        num_scalar_prefetch=1, grid=(n_blocks + 1,),
        in_specs=[pl.BlockSpec((1,) + table.shape[1:], lambda i, be: (i, 0, 0),
                               memory_space=pltpu.SMEM),
                  pl.BlockSpec(memory_space=pl.ANY),
                  per_expert(d, 2 * D_FF), per_expert(1, 2 * D_FF),
                  per_expert(D_FF, d), per_expert(1, d)],
        out_specs=pl.BlockSpec(memory_space=pl.ANY),
        scratch_shapes=[pltpu.VMEM((N_ROWBUF, MOE_BLOCK, sub, lanes), F32),
                        pltpu.VMEM((N_ROWBUF, MOE_BLOCK, sub, lanes), F32),
                        pltpu.VMEM((d, 2 * D_FF), BF16), pltpu.VMEM((D_FF, d), BF16),
                        pltpu.SemaphoreType.DMA((N_ROWBUF,)),
                        pltpu.SemaphoreType.DMA((N_ROWBUF,))])
    return pl.pallas_call(
        functools.partial(_moe_body, n_blocks=n_blocks),
        out_shape=jax.ShapeDtypeStruct(((n_blocks + 1) * MOE_BLOCK, sub, lanes), F32),
        grid_spec=grid_spec, compiler_params=_params("arbitrary"), name="moe")(
            block_e, table, h1t,
            w_gate_up, shape4(b_gate_up), w_down, shape4(b_down))


def _combine_body(y0_ref, y1_ref, y2_ref, y3_ref, tw_ref, h_ref, g_ref, b_ref, o_ref, *, alpha):
    tw = tw_ref[...]
    ffn = None
    for j, y_ref in enumerate((y0_ref, y1_ref, y2_ref, y3_ref)):
        term = tw[:, j:j + 1] * y_ref[...].reshape(h_ref.shape)
        ffn = term if ffn is None else ffn + term
    o_ref[...] = _layer_norm(alpha * h_ref[...] + ffn, g_ref[...], b_ref[...])


def _combine_call(ys, top_w, h1, g, b, alpha):
    n, d = h1.shape
    tm = TM_POST
    tiles = n // tm
    row = pl.BlockSpec((tm, d), lambda i: (i, 0))
    y_specs = [pl.BlockSpec((tm,) + ys.shape[1:], lambda i, j=j: (j * tiles + i, 0, 0))
               for j in range(TOP_K)]
    return pl.pallas_call(
        functools.partial(_combine_body, alpha=alpha),
        out_shape=jax.ShapeDtypeStruct((n, d), F32), grid=(tiles,),
        in_specs=y_specs + [pl.BlockSpec((tm, TOP_K), lambda i: (i, 0)), row,
                            _const_spec((1, d)), _const_spec((1, d))],
        out_specs=row, compiler_params=_params("parallel"), name="combine")(
            ys, ys, ys, ys, top_w, h1, g, b)


def _layer_weights(l, w_in, conv_qkv, conv_mix, a_log, dt_bias, gdn_norm, w_mem_kv, w_br, w_o,
                   ln1_g, ln1_b, w_router, b_router, ln2_g, ln2_b):
    d = D_MODEL
    wi = w_in[l].astype(BF16)
    c0 = 3 * WIDTH
    c_z = c0 + 2 * N_HEADS
    c_x = c_z + WIDTH
    c_gl = c_x + 4 * WIDTH
    wab = jnp.zeros((d, LANES), BF16).at[:, :2 * N_HEADS].set(wi[:, c0:c_z])
    neg_a = -jnp.exp(a_log[l].astype(F32))
    dtb = dt_bias[l].astype(F32)
    abp = jnp.zeros((2, LANES), F32).at[0, :N_HEADS].set(neg_a).at[1, :N_HEADS].set(dtb)
    abpt = jnp.zeros((2 * N_HEADS, 2), F32).at[:N_HEADS, 0].set(neg_a).at[:N_HEADS, 1].set(dtb)
    cq = conv_qkv[l].astype(F32)
    return dict(
        wq=wi[:, 0:WIDTH], wk=wi[:, WIDTH:2 * WIDTH], wv=wi[:, 2 * WIDTH:c0],
        wab=wab, wabt=wi[:, c0:c_z].T, wz=wi[:, c_z:c_x],
        wxc=wi[:, c_x:c_x + WIDTH], wgb=wi[:, c_x + WIDTH:c_x + 2 * WIDTH],
        wgc=wi[:, c_x + 2 * WIDTH:c_x + 3 * WIDTH], wqm=wi[:, c_x + 3 * WIDTH:c_gl],
        wgl=wi[:, c_gl:],
        cq=cq[:, 0:WIDTH], ck=cq[:, WIDTH:2 * WIDTH], cv=cq[:, 2 * WIDTH:],
        cmix=conv_mix[l].astype(F32), abp=abp, abpt=abpt,
        wkv=w_mem_kv[l].astype(BF16),
        wb0=w_br[l, 0].astype(BF16), wb1=w_br[l, 1].astype(BF16), wb2=w_br[l, 2].astype(BF16),
        wo=w_o[l].astype(BF16), gn=gdn_norm[l].astype(F32).reshape(1, HEAD_DIM),
        ln1_g=ln1_g[l].reshape(1, d), ln1_b=ln1_b[l].reshape(1, d),
        wr=w_router[l].astype(F32).T, br=b_router[l].astype(F32).reshape(N_EXPERTS, 1),
        ln2_g=ln2_g[l].reshape(1, d), ln2_b=ln2_b[l].reshape(1, d),
    )


def kernel(x, mem, ln0_g, ln0_b, w_in, conv_qkv, conv_mix, a_log, dt_bias, gdn_norm, w_mem_kv, w_br, w_o, ln1_g, ln1_b, w_router, b_router, w_gate_up, b_gate_up, w_down, b_down, ln2_g, ln2_b):
    batch, seq_len, d = x.shape
    depth = w_in.shape[0]
    n_tok = batch * seq_len
    alpha = (2 * depth) ** 0.25
    n_assign = n_tok * TOP_K
    n_blocks = (n_assign + N_EXPERTS * (MOE_BLOCK - 1) + MOE_BLOCK - 1) // MOE_BLOCK
    assert d == D_MODEL and seq_len % TM_PRE == 0 and seq_len % TT_GDN == 0
    assert n_tok % TM_LN == 0 and n_tok % TM_POST == 0

    h = _ln_call(x.reshape(n_tok, d), ln0_g, ln0_b)
    for l in range(depth):
        w = _layer_weights(l, w_in, conv_qkv, conv_mix, a_log, dt_bias, gdn_norm, w_mem_kv, w_br,
                           w_o, ln1_g, ln1_b, w_router, b_router, ln2_g, ln2_b)
        km, vm = _memkv_call(mem, w["wkv"])
        q, k, v, gcol, grow, zs, part, g0 = _pre_call(h, km, vm, w, seq_len)
        o = _gdn_call(q, k, v, gcol, grow, batch, seq_len)
        h1, h1t, top_idx, top_w = _post_call(o, zs, g0, part, h, w, alpha)
        table, block_e = _route(top_idx, n_blocks)
        ys = _moe_call(h1t, table, block_e, l, w_gate_up, b_gate_up, w_down, b_down)
        h = _combine_call(ys, top_w.T, h1, w["ln2_g"], w["ln2_b"], alpha)
    return h.reshape(batch, seq_len, d)
```

```python
import functools

import jax
import jax.numpy as jnp
from jax import lax
from jax.experimental import pallas as pl
from jax.experimental.pallas import tpu as pltpu

F32 = jnp.float32
BF16 = jnp.bfloat16
HIGHEST = lax.Precision.HIGHEST

D_MODEL = 1024
N_HEADS = 4
HEAD_DIM = 128
WIDTH = N_HEADS * HEAD_DIM
GDN_CONV = 4
SHORT_CONV = 3
N_EXPERTS = 32
TOP_K = 4
D_FF = 1024
SWIGLU_LIMIT = 7.0
SWIGLU_ALPHA = 1.702
MOE_BLOCK = 256
N_ROWBUF = 3
MOE_FF_SPLIT = 4
LN_EPS = 1e-5
RMS_EPS = 1e-6
HALO = 8
LANES = 128
VMEM_LIMIT = 56 * 1024 * 1024

TM_PRE = 512
TM_POST = 512
TM_LN = 512
TT_GDN = 256
GDN_CHUNK = 64


def _dot(a, b, precision=None):
    return jnp.dot(a, b, preferred_element_type=F32, precision=precision)


def _dot_nt(a, b, precision=None):
    return lax.dot_general(a, b, (((1,), (1,)), ((), ())), preferred_element_type=F32,
                           precision=precision)


def _dot_tn(a, b):
    return lax.dot_general(a, b, (((0,), (0,)), ((), ())), preferred_element_type=F32)


def _layer_norm(y, g, b):
    mu = jnp.mean(y, axis=-1, keepdims=True)
    d = y - mu
    var = jnp.mean(d * d, axis=-1, keepdims=True)
    return d * lax.rsqrt(var + LN_EPS) * g + b


def _sigmoid(x):
    return 0.5 * jnp.tanh(0.5 * x) + 0.5


def _silu(x):
    return x * _sigmoid(x)


def _softplus(x):
    return jnp.maximum(x, 0.0) + jnp.log(1.0 + jnp.exp(-jnp.abs(x)))


def _const_spec(shape):
    zeros = (0,) * len(shape)
    return pl.BlockSpec(shape, lambda *_: zeros, pipeline_mode=pl.Buffered(1))


def _params(*sem):
    return pltpu.CompilerParams(dimension_semantics=sem, vmem_limit_bytes=VMEM_LIMIT)


def _ln_body(x_ref, g_ref, b_ref, o_ref):
    o_ref[...] = _layer_norm(x_ref[...], g_ref[...], b_ref[...])


def _ln_call(x, g, b):
    n, d = x.shape
    row = pl.BlockSpec((TM_LN, d), lambda i: (i, 0))
    return pl.pallas_call(
        _ln_body, out_shape=jax.ShapeDtypeStruct((n, d), F32), grid=(n // TM_LN,),
        in_specs=[row, _const_spec((1, d)), _const_spec((1, d))], out_specs=row,
        compiler_params=_params("parallel"), name="ln0")(x, g.reshape(1, d), b.reshape(1, d))


def _memkv_body(mem_ref, w_ref, k_ref, v_ref):
    kv = _dot(mem_ref[0].astype(BF16), w_ref[...])
    k_ref[0] = kv[:, :WIDTH].astype(BF16)
    v_ref[0] = kv[:, WIDTH:].astype(BF16)


def _memkv_call(mem, w_kv):
    b, m, d = mem.shape
    out = jax.ShapeDtypeStruct((b, m, WIDTH), BF16)
    blk = pl.BlockSpec((1, m, WIDTH), lambda i: (i, 0, 0))
    return pl.pallas_call(
        _memkv_body, out_shape=(out, out), grid=(b,),
        in_specs=[pl.BlockSpec((1, m, d), lambda i: (i, 0, 0)), _const_spec((d, 2 * WIDTH))],
        out_specs=(blk, blk), compiler_params=_params("parallel"), name="memkv")(mem, w_kv)


def _causal_conv(x, halo_ref, buf_ref, w_ref, first_tile):
    tm = x.shape[0]
    taps = w_ref.shape[0]

    @pl.when(first_tile)
    def _():
        halo_ref[...] = jnp.zeros_like(halo_ref)

    buf_ref[0:HALO, :] = halo_ref[...]
    buf_ref[HALO:HALO + tm, :] = x
    halo_ref[...] = x[tm - HALO:, :]
    acc = w_ref[taps - 1:taps, :] * x
    for j in range(taps - 1):
        off = HALO - (taps - 1) + j
        acc = acc + w_ref[j:j + 1, :] * buf_ref[off:off + tm, :]
    return acc


def _pre_body(h_ref, wq_ref, wk_ref, wv_ref, wab_ref, wabt_ref, wz_ref, wxc_ref, wgb_ref,
              wgc_ref, wqm_ref, wgl_ref, cq_ref, ck_ref, cv_ref, cmix_ref, abp_ref, abpt_ref,
              km_ref, vm_ref, wb1_ref, wb2_ref,
              q_ref, k_ref, v_ref, gcol_ref, grow_ref, zs_ref, part_ref, g0_ref,
              hq_ref, hk_ref, hv_ref, hx_ref, bufq_ref, bufk_ref, bufv_ref, bufx_ref, *,
              tiles_per_seq):
    first = pl.program_id(0) % tiles_per_seq == 0
    hb = h_ref[...].astype(BF16)

    def head_l2(x, scale):
        outs = []
        for hd in range(N_HEADS):
            xh = x[:, hd * HEAD_DIM:(hd + 1) * HEAD_DIM]
            ss = jnp.sum(xh * xh, axis=-1, keepdims=True)
            outs.append(xh * (lax.rsqrt(ss + RMS_EPS) * scale))
        return jnp.concatenate(outs, axis=-1)

    proj_q = _dot(hb, wq_ref[...])
    proj_k = _dot(hb, wk_ref[...])
    q = _silu(_causal_conv(proj_q, hq_ref, bufq_ref, cq_ref, first))
    q_ref[...] = head_l2(q, HEAD_DIM ** -0.5)
    proj_v = _dot(hb, wv_ref[...])
    k = _silu(_causal_conv(proj_k, hk_ref, bufk_ref, ck_ref, first))
    k_ref[...] = head_l2(k, 1.0)
    proj_z = _dot(hb, wz_ref[...])
    v_ref[...] = _silu(_causal_conv(proj_v, hv_ref, bufv_ref, cv_ref, first))
    xc = _dot(hb, wxc_ref[...])
    gc = _dot(hb, wgc_ref[...])
    zs_ref[...] = _silu(proj_z)
    proj_gb = _dot(hb, wgb_ref[...])

    ab = _dot(hb, wab_ref[...])
    a_scale = abp_ref[0:1, :]
    a_bias = abp_ref[1:2, :]
    lane = lax.broadcasted_iota(jnp.int32, ab.shape, 1)
    gb_col = jnp.where(lane < N_HEADS, a_scale * _softplus(ab + a_bias), _sigmoid(ab))
    gcol_ref[...] = gb_col[:, :2 * N_HEADS]
    abt = _dot_nt(wabt_ref[...], hb)
    sub = lax.broadcasted_iota(jnp.int32, abt.shape, 0)
    grow_ref[...] = jnp.where(sub < N_HEADS,
                              abpt_ref[:, 0:1] * _softplus(abt + abpt_ref[:, 1:2]), _sigmoid(abt))

    qm = _dot(hb, wqm_ref[...]).astype(BF16)
    y_b = proj_gb * _causal_conv(gc * xc, hx_ref, bufx_ref, cmix_ref, first)
    gl0 = _dot(hb, wgl_ref[:, 0:D_MODEL])
    p_b = _dot(y_b.astype(BF16), wb1_ref[...])

    heads = []
    for hd in range(N_HEADS):
        sl = slice(hd * HEAD_DIM, (hd + 1) * HEAD_DIM)
        s = _dot_nt(qm[:, sl], km_ref[0, :, sl]) * (HEAD_DIM ** -0.5)
        e = jnp.exp(s - jnp.max(s, axis=-1, keepdims=True))
        p = e / jnp.sum(e, axis=-1, keepdims=True)
        heads.append(_dot(p.astype(BF16), vm_ref[0, :, sl]))
    y_m = jnp.concatenate(heads, axis=-1)
    gl1 = _dot(hb, wgl_ref[:, D_MODEL:2 * D_MODEL])
    g0_ref[...] = _sigmoid(gl0)

    p_m = _dot(y_m.astype(BF16), wb2_ref[...])
    gl2 = _dot(hb, wgl_ref[:, 2 * D_MODEL:3 * D_MODEL])
    part_ref[...] = _sigmoid(gl1) * p_b + _sigmoid(gl2) * p_m


def _pre_call(h, km, vm, w, seq_len):
    n, d = h.shape
    tm = TM_PRE
    tiles_per_seq = seq_len // tm
    row = lambda c: pl.BlockSpec((tm, c), lambda i: (i, 0))
    mem_spec = pl.BlockSpec((1,) + km.shape[1:], lambda i: (i // tiles_per_seq, 0, 0))
    consts = [w["wq"], w["wk"], w["wv"], w["wab"], w["wabt"], w["wz"], w["wxc"], w["wgb"],
              w["wgc"], w["wqm"], w["wgl"], w["cq"], w["ck"], w["cv"], w["cmix"], w["abp"],
              w["abpt"]]
    tail = [w["wb1"], w["wb2"]]
    out_shape = (
        jax.ShapeDtypeStruct((n, WIDTH), F32),
        jax.ShapeDtypeStruct((n, WIDTH), F32),
        jax.ShapeDtypeStruct((n, WIDTH), F32),
        jax.ShapeDtypeStruct((n, 2 * N_HEADS), F32),
        jax.ShapeDtypeStruct((2 * N_HEADS, n), F32),
        jax.ShapeDtypeStruct((n, WIDTH), F32),
        jax.ShapeDtypeStruct((n, d), F32),
        jax.ShapeDtypeStruct((n, d), F32),
    )
    out_specs = (row(WIDTH), row(WIDTH), row(WIDTH), row(2 * N_HEADS),
                 pl.BlockSpec((2 * N_HEADS, tm), lambda i: (0, i)), row(WIDTH), row(d), row(d))
    scratch = [pltpu.VMEM((HALO, WIDTH), F32)] * 4 + [pltpu.VMEM((HALO + tm, WIDTH), F32)] * 4
    return pl.pallas_call(
        functools.partial(_pre_body, tiles_per_seq=tiles_per_seq),
        out_shape=out_shape, grid=(n // tm,),
        in_specs=[row(d)] + [_const_spec(c.shape) for c in consts] + [mem_spec, mem_spec]
        + [_const_spec(c.shape) for c in tail],
        out_specs=out_specs, scratch_shapes=scratch,
        compiler_params=_params("arbitrary"), name="mixer_pre")(h, *consts, km, vm, *tail)


def _gdn_body(q_ref, k_ref, v_ref, gcol_ref, grow_ref, o_ref, s_ref, *, chunk):
    @pl.when(pl.program_id(1) == 0)
    def _():
        s_ref[...] = jnp.zeros_like(s_ref)

    tt = q_ref.shape[0]
    n_chunks = tt // chunk
    shift = chunk.bit_length() - 1
    row = lax.broadcasted_iota(jnp.int32, (tt, tt), 0)
    col = lax.broadcasted_iota(jnp.int32, (tt, tt), 1)
    same = (row >> shift) == (col >> shift)
    causal = jnp.logical_and(same, row >= col)
    strict = jnp.logical_and(same, row > col)
    eye = (row == col).astype(F32)
    causal_f = causal.astype(F32)
    same_f = same.astype(F32)
    n_doublings = chunk.bit_length() - 2

    gcol = gcol_ref[...]
    grow = grow_ref[...]
    gcum_col = _dot(causal_f, gcol, precision=HIGHEST)
    gcum_row = _dot_nt(grow, causal_f, precision=HIGHEST)
    gtot_col = _dot(same_f, gcol, precision=HIGHEST)
    gtot_row = _dot(grow, same_f, precision=HIGHEST)

    heads = []
    for hd in range(N_HEADS):
        lanes = slice(hd * HEAD_DIM, (hd + 1) * HEAD_DIM)
        k = k_ref[:, lanes]
        gc = gcum_col[:, hd:hd + 1]
        gr = gcum_row[hd:hd + 1, :]
        beta = gcol[:, N_HEADS + hd:N_HEADS + hd + 1]
        decay = jnp.exp(jnp.where(causal, gc - gr, -1e30))
        kb = k * beta
        k16 = k.astype(BF16)
        low = jnp.where(strict, _dot_nt(kb.astype(BF16), k16) * decay, 0.0)
        heads.append(dict(gc=gc, beta=beta, decay=decay, e_gc=jnp.exp(gc), k=k, kb=kb, k16=k16,
                          inv=eye - low, power=low))
    for _ in range(n_doublings):
        for head in heads:
            p16 = head["power"].astype(BF16)
            head["power"] = _dot(p16, p16)
        for head in heads:
            head["inv"] = head["inv"] + _dot(head["inv"].astype(BF16), head["power"].astype(BF16))
    for hd, head in enumerate(heads):
        lanes = slice(hd * HEAD_DIM, (hd + 1) * HEAD_DIM)
        q = q_ref[:, lanes]
        rhs = jnp.concatenate([v_ref[:, lanes] * head["beta"], head["kb"] * head["e_gc"]],
                              axis=-1).astype(BF16)
        uw = _dot(head["inv"].astype(BF16), rhs)
        head.update(u=uw[:, :HEAD_DIM], w=uw[:, HEAD_DIM:], qd=q * head["e_gc"],
                    ke=head["k"] * jnp.exp(gtot_col[:, hd:hd + 1] - head["gc"]),
                    qk=(_dot_nt(q.astype(BF16), head["k16"]) * head["decay"]).astype(BF16))

    pair = 2 * HEAD_DIM
    n_pairs = N_HEADS // 2
    prow = lax.broadcasted_iota(jnp.int32, (pair, pair), 0)
    pcol = lax.broadcasted_iota(jnp.int32, (pair, pair), 1)
    block_diag = (prow >= HEAD_DIM) == (pcol >= HEAD_DIM)
    first_head = lax.broadcasted_iota(jnp.int32, (1, pair), 1) < HEAD_DIM
    two = lambda p, name: jnp.concatenate([heads[2 * p][name], heads[2 * p + 1][name]], axis=-1)
    u2, w2, qd2, ke2 = ([two(p, name) for p in range(n_pairs)] for name in ("u", "w", "qd", "ke"))
    state = [s_ref[p] for p in range(n_pairs)]
    o_state = [[] for _ in range(n_pairs)]
    v_new = [[] for _ in range(n_pairs)]
    for ci in range(n_chunks):
        rows = slice(ci * chunk, (ci + 1) * chunk)
        at = slice(ci * chunk, ci * chunk + 1)
        for p in range(n_pairs):
            lhs = jnp.concatenate([w2[p][rows], qd2[p][rows]], axis=0).astype(BF16)
            res = _dot(lhs, state[p].astype(BF16))
            vn = u2[p][rows] - res[:chunk]
            o_state[p].append(res[chunk:])
            v_new[p].append(vn)
            chunk_decay = jnp.exp(jnp.where(first_head, gtot_row[2 * p:2 * p + 1, at],
                                            gtot_row[2 * p + 1:2 * p + 2, at]))
            outer = _dot_tn(ke2[p][rows].astype(BF16), vn.astype(BF16))
            state[p] = state[p] * chunk_decay + jnp.where(block_diag, outer, 0.0)
    for p in range(n_pairs):
        s_ref[p] = state[p]
        o_pair = jnp.concatenate(o_state[p], axis=0)
        v_pair = jnp.concatenate(v_new[p], axis=0).astype(BF16)
        for j in range(2):
            lanes = slice(j * HEAD_DIM, (j + 1) * HEAD_DIM)
            out_lanes = slice((2 * p + j) * HEAD_DIM, (2 * p + j + 1) * HEAD_DIM)
            o_ref[:, out_lanes] = o_pair[:, lanes] + _dot(heads[2 * p + j]["qk"], v_pair[:, lanes])


def _gdn_call(q, k, v, gcol, grow, batch, seq_len):
    n = q.shape[0]
    tt = TT_GDN
    steps = seq_len // tt
    tok = pl.BlockSpec((tt, WIDTH), lambda b, t: (b * steps + t, 0))
    return pl.pallas_call(
        functools.partial(_gdn_body, chunk=GDN_CHUNK),
        out_shape=jax.ShapeDtypeStruct((n, WIDTH), F32), grid=(batch, steps),
        in_specs=[tok, tok, tok,
                  pl.BlockSpec((tt, 2 * N_HEADS), lambda b, t: (b * steps + t, 0)),
                  pl.BlockSpec((2 * N_HEADS, tt), lambda b, t: (0, b * steps + t))],
        out_specs=tok,
        scratch_shapes=[pltpu.VMEM((N_HEADS // 2, 2 * HEAD_DIM, 2 * HEAD_DIM), F32)],
        compiler_params=_params("parallel", "arbitrary"), name="gdn")(q, k, v, gcol, grow)


def _post_body(o_ref, zs_ref, g0_ref, part_ref, h_ref, gn_ref, wb0_ref, wo_ref, lg_ref, lb_ref,
               wr_ref, br_ref, h1_ref, h1t_ref, idx_ref, tw_ref, *, alpha):
    o = o_ref[...]
    zs = zs_ref[...]
    heads = []
    for hd in range(N_HEADS):
        sl = slice(hd * HEAD_DIM, (hd + 1) * HEAD_DIM)
        oh = o[:, sl]
        ms = jnp.mean(oh * oh, axis=-1, keepdims=True)
        heads.append(oh * lax.rsqrt(ms + RMS_EPS) * gn_ref[...] * zs[:, sl])
    y_a = jnp.concatenate(heads, axis=-1).astype(BF16)
    merged = g0_ref[...] * _dot(y_a, wb0_ref[...]) + part_ref[...]
    mix = _dot(merged.astype(BF16), wo_ref[...])
    h1 = _layer_norm(alpha * h_ref[...] + mix, lg_ref[...], lb_ref[...])
    h1_ref[...] = h1
    h1t_ref[...] = h1.reshape(h1.shape[0], D_MODEL // LANES, LANES)

    logits = _dot_nt(wr_ref[...], h1, precision=HIGHEST) + br_ref[...]
    expert = lax.broadcasted_iota(jnp.int32, logits.shape, 0)
    idx, top = [], []
    for _ in range(TOP_K):
        m = jnp.max(logits, axis=0, keepdims=True)
        sel = jnp.min(jnp.where(logits == m, expert, N_EXPERTS), axis=0, keepdims=True)
        idx.append(sel)
        top.append(m)
        logits = jnp.where(expert == sel, -jnp.inf, logits)
    e = jnp.exp(jnp.concatenate(top, axis=0) - top[0])
    idx_ref[...] = jnp.concatenate(idx, axis=0)
    tw_ref[...] = e / jnp.sum(e, axis=0, keepdims=True)


def _post_call(o, zs, g0, part, h, w, alpha):
    n, d = h.shape
    tm = TM_POST
    row = lambda c: pl.BlockSpec((tm, c), lambda i: (i, 0))
    consts = [w["gn"], w["wb0"], w["wo"], w["ln1_g"], w["ln1_b"], w["wr"], w["br"]]
    return pl.pallas_call(
        functools.partial(_post_body, alpha=alpha),
        out_shape=(jax.ShapeDtypeStruct((n, d), F32),
                   jax.ShapeDtypeStruct((n, d // LANES, LANES), F32),
                   jax.ShapeDtypeStruct((TOP_K, n), jnp.int32),
                   jax.ShapeDtypeStruct((TOP_K, n), F32)),
        grid=(n // tm,),
        in_specs=[row(WIDTH), row(WIDTH), row(d), row(d), row(d)]
        + [_const_spec(c.shape) for c in consts],
        out_specs=(row(d), pl.BlockSpec((tm, d // LANES, LANES), lambda i: (i, 0, 0)),
                   pl.BlockSpec((TOP_K, tm), lambda i: (0, i)),
                   pl.BlockSpec((TOP_K, tm), lambda i: (0, i))),
        compiler_params=_params("parallel"), name="mixer_post")(o, zs, g0, part, h, *consts)


def _route(top_idx, n_blocks):
    n_tok = top_idx.shape[1]
    n_assign = n_tok * TOP_K
    n_slots = n_blocks * MOE_BLOCK
    flat_e = top_idx.T.reshape(n_assign)
    seg = 256
    onehot = (flat_e[:, None] == jnp.arange(N_EXPERTS, dtype=jnp.int32)[None, :]).astype(F32)
    oh3 = onehot.reshape(n_assign // seg, seg, N_EXPERTS)
    within = jnp.einsum("ij,bje->bie", jnp.tril(jnp.ones((seg, seg), F32)), oh3)
    seg_tot = within[:, -1, :]
    seg_off = jnp.cumsum(seg_tot, axis=0) - seg_tot
    rank = (jnp.sum(oh3 * (within + seg_off[:, None, :]), axis=-1) - 1.0).astype(jnp.int32)
    rank = rank.reshape(n_assign)
    counts = (seg_off[-1] + seg_tot[-1]).astype(jnp.int32)
    padded = (counts + MOE_BLOCK - 1) // MOE_BLOCK * MOE_BLOCK
    pad_end = jnp.cumsum(padded)
    dest = (pad_end - padded)[flat_e] + rank
    slot_a = jnp.full((n_slots,), -1, jnp.int32).at[dest].set(
        jnp.arange(n_assign, dtype=jnp.int32), unique_indices=True, mode="promise_in_bounds")
    is_pad = slot_a < 0
    pad_rank = jnp.cumsum(is_pad.astype(jnp.int32)) - 1
    a = jnp.maximum(slot_a, 0)
    slot_tok = a // TOP_K
    slot_dst = jnp.where(is_pad, n_assign + pad_rank, (a % TOP_K) * n_tok + slot_tok)
    block_start = jnp.arange(n_blocks + 1, dtype=jnp.int32) * MOE_BLOCK
    block_e = jnp.minimum(jnp.sum((pad_end[None, :] <= block_start[:, None]).astype(jnp.int32),
                                  axis=1), N_EXPERTS - 1)
    filler_tok = jnp.zeros((3 * MOE_BLOCK,), jnp.int32)
    filler_dst = n_slots + jnp.arange(MOE_BLOCK, dtype=jnp.int32)
    tok_blocks = jnp.concatenate([slot_tok.astype(jnp.int32), filler_tok]).reshape(-1, MOE_BLOCK)
    dst_blocks = jnp.concatenate([filler_dst, slot_dst.astype(jnp.int32)]).reshape(-1, MOE_BLOCK)
    steps = n_blocks + 1
    table = jnp.stack([tok_blocks[2:], dst_blocks,
                       jnp.broadcast_to(tok_blocks[0], (steps, MOE_BLOCK)),
                       jnp.broadcast_to(tok_blocks[1], (steps, MOE_BLOCK))], axis=1)
    return table, block_e


TAB_GATHER, TAB_SCATTER, TAB_FIRST, TAB_SECOND = range(4)


def _moe_body(be_ref, tab_ref, h_hbm, wgu_ref, bgu_ref, wd_ref,
              bd_ref, ys_hbm, xbuf, ybuf, xin, wgu16, wd16, gsem, ssem, *, n_blocks):
    i = pl.program_id(0)
    cur = i % N_ROWBUF
    nxt = (i + 2) % N_ROWBUF

    def gather_row(table_row, r, slot):
        return pltpu.make_async_copy(h_hbm.at[tab_ref[0, table_row, r]], xbuf.at[slot, r],
                                     gsem.at[slot])

    def gather_wait(slot):
        pltpu.make_async_copy(h_hbm.at[pl.ds(0, MOE_BLOCK)], xbuf.at[slot],
                              gsem.at[slot]).wait()

    def scatter_wait(slot):
        pltpu.make_async_copy(ybuf.at[slot], ys_hbm.at[pl.ds(0, MOE_BLOCK)],
                              ssem.at[slot]).wait()

    @pl.when(i == 0)
    def _():
        ybuf[...] = jnp.zeros_like(ybuf)
        for r in range(MOE_BLOCK):
            gather_row(TAB_FIRST, r, 0).start()
        for r in range(MOE_BLOCK):
            gather_row(TAB_SECOND, r, 1).start()

    @pl.when(jnp.logical_or(i == 0, be_ref[i] != be_ref[jnp.maximum(i - 1, 0)]))
    def _():
        wgu16[...] = wgu_ref[0, 0].astype(BF16)
        wd16[...] = wd_ref[0, 0].astype(BF16)

    gather_wait(cur)

    @pl.when(i >= 2)
    def _():
        scatter_wait(cur)

    @pl.when(i <= n_blocks)
    def _():
        xin[...] = xbuf[cur].reshape(MOE_BLOCK, D_MODEL).astype(BF16)

    fc = D_FF // MOE_FF_SPLIT
    rows_per_issue = MOE_BLOCK // MOE_FF_SPLIT

    def gate_up(j):
        x16 = xin[...]
        gcols = slice(j * fc, (j + 1) * fc)
        ucols = slice(D_FF + j * fc, D_FF + (j + 1) * fc)
        return (_dot(x16, wgu16[:, gcols]) + bgu_ref[0, 0, :, gcols],
                _dot(x16, wgu16[:, ucols]) + bgu_ref[0, 0, :, ucols])

    acc = None
    ahead = gate_up(0)
    for j in range(MOE_FF_SPLIT):
        lo = j * rows_per_issue
        for r in range(lo, lo + rows_per_issue):
            gather_row(TAB_GATHER, r, nxt).start()
        for r in range(lo, lo + rows_per_issue):
            pltpu.make_async_copy(ybuf.at[nxt, r], ys_hbm.at[tab_ref[0, TAB_SCATTER, r]],
                                  ssem.at[nxt]).start()
        gate, up = ahead
        if j + 1 < MOE_FF_SPLIT:
            ahead = gate_up(j + 1)
        gate = jnp.minimum(gate, SWIGLU_LIMIT)
        up = jnp.clip(up, -SWIGLU_LIMIT, SWIGLU_LIMIT)
        act = (up + 1.0) * (gate * _sigmoid(gate * SWIGLU_ALPHA))
        part = _dot(act.astype(BF16), wd16[j * fc:(j + 1) * fc, :])
        xbuf[cur, 0] = part[0:8, 0:LANES]
        ybuf[cur, 0] = xbuf[cur, 0]
        head = jnp.concatenate([ybuf[cur, 0], part[0:8, LANES:]], axis=1)
        part = jnp.concatenate([head, part[8:]], axis=0)
        acc = part if acc is None else acc + part
    ybuf[cur] = (acc + bd_ref[0, 0]).reshape(MOE_BLOCK, D_MODEL // LANES, LANES)

    @pl.when(i == n_blocks)
    def _():
        gather_wait((i + 1) % N_ROWBUF)
        gather_wait(nxt)
        scatter_wait((i + 1) % N_ROWBUF)
        scatter_wait(nxt)


def _moe_call(h1t, table, block_e, layer, w_gate_up, b_gate_up, w_down, b_down):
    n, sub, lanes = h1t.shape
    d = sub * lanes
    n_blocks = block_e.shape[0] - 1
    per_expert = lambda r, c: pl.BlockSpec((1, 1, r, c), lambda i, be: (layer, be[i], 0, 0))
    shape4 = lambda b: b.reshape(b.shape[0], N_EXPERTS, 1, b.shape[-1])
    grid_spec = pltpu.PrefetchScalarGridSpec(
        num_scalar_prefetch=1, grid=(n_blocks + 1,),
        in_specs=[pl.BlockSpec((1,) + table.shape[1:], lambda i, be: (i, 0, 0),
                               memory_space=pltpu.SMEM),
                  pl.BlockSpec(memory_space=pl.ANY),
                  per_expert(d, 2 * D_FF), per_expert(1, 2 * D_FF),
                  per_expert(D_FF, d), per_expert(1, d)],
        out_specs=pl.BlockSpec(memory_space=pl.ANY),
        scratch_shapes=[pltpu.VMEM((N_ROWBUF, MOE_BLOCK, sub, lanes), F32),
                        pltpu.VMEM((N_ROWBUF, MOE_BLOCK, sub, lanes), F32),
                        pltpu.VMEM((MOE_BLOCK, d), BF16),
                        pltpu.VMEM((d, 2 * D_FF), BF16), pltpu.VMEM((D_FF, d), BF16),
                        pltpu.SemaphoreType.DMA((N_ROWBUF,)),
                        pltpu.SemaphoreType.DMA((N_ROWBUF,))])
    return pl.pallas_call(
        functools.partial(_moe_body, n_blocks=n_blocks),
        out_shape=jax.ShapeDtypeStruct(((n_blocks + 1) * MOE_BLOCK, sub, lanes), F32),
        grid_spec=grid_spec, compiler_params=_params("arbitrary"), name="moe")(
            block_e, table, h1t,
            w_gate_up, shape4(b_gate_up), w_down, shape4(b_down))


def _combine_body(y0_ref, y1_ref, y2_ref, y3_ref, tw_ref, h_ref, g_ref, b_ref, o_ref, *, alpha):
    tw = tw_ref[...]
    ffn = None
    for j, y_ref in enumerate((y0_ref, y1_ref, y2_ref, y3_ref)):
        term = tw[:, j:j + 1] * y_ref[...].reshape(h_ref.shape)
        ffn = term if ffn is None else ffn + term
    o_ref[...] = _layer_norm(alpha * h_ref[...] + ffn, g_ref[...], b_ref[...])


def _combine_call(ys, top_w, h1, g, b, alpha):
    n, d = h1.shape
    tm = TM_POST
    tiles = n // tm
    row = pl.BlockSpec((tm, d), lambda i: (i, 0))
    y_specs = [pl.BlockSpec((tm,) + ys.shape[1:], lambda i, j=j: (j * tiles + i, 0, 0))
               for j in range(TOP_K)]
    return pl.pallas_call(
        functools.partial(_combine_body, alpha=alpha),
        out_shape=jax.ShapeDtypeStruct((n, d), F32), grid=(tiles,),
        in_specs=y_specs + [pl.BlockSpec((tm, TOP_K), lambda i: (i, 0)), row,
                            _const_spec((1, d)), _const_spec((1, d))],
        out_specs=row, compiler_params=_params("parallel"), name="combine")(
            ys, ys, ys, ys, top_w, h1, g, b)


def _layer_weights(l, w_in, conv_qkv, conv_mix, a_log, dt_bias, gdn_norm, w_mem_kv, w_br, w_o,
                   ln1_g, ln1_b, w_router, b_router, ln2_g, ln2_b):
    d = D_MODEL
    wi = w_in[l].astype(BF16)
    c0 = 3 * WIDTH
    c_z = c0 + 2 * N_HEADS
    c_x = c_z + WIDTH
    c_gl = c_x + 4 * WIDTH
    wab = jnp.zeros((d, LANES), BF16).at[:, :2 * N_HEADS].set(wi[:, c0:c_z])
    neg_a = -jnp.exp(a_log[l].astype(F32))
    dtb = dt_bias[l].astype(F32)
    abp = jnp.zeros((2, LANES), F32).at[0, :N_HEADS].set(neg_a).at[1, :N_HEADS].set(dtb)
    abpt = jnp.zeros((2 * N_HEADS, 2), F32).at[:N_HEADS, 0].set(neg_a).at[:N_HEADS, 1].set(dtb)
    cq = conv_qkv[l].astype(F32)
    return dict(
        wq=wi[:, 0:WIDTH], wk=wi[:, WIDTH:2 * WIDTH], wv=wi[:, 2 * WIDTH:c0],
        wab=wab, wabt=wi[:, c0:c_z].T, wz=wi[:, c_z:c_x],
        wxc=wi[:, c_x:c_x + WIDTH], wgb=wi[:, c_x + WIDTH:c_x + 2 * WIDTH],
        wgc=wi[:, c_x + 2 * WIDTH:c_x + 3 * WIDTH], wqm=wi[:, c_x + 3 * WIDTH:c_gl],
        wgl=wi[:, c_gl:],
        cq=cq[:, 0:WIDTH], ck=cq[:, WIDTH:2 * WIDTH], cv=cq[:, 2 * WIDTH:],
        cmix=conv_mix[l].astype(F32), abp=abp, abpt=abpt,
        wkv=w_mem_kv[l].astype(BF16),
        wb0=w_br[l, 0].astype(BF16), wb1=w_br[l, 1].astype(BF16), wb2=w_br[l, 2].astype(BF16),
        wo=w_o[l].astype(BF16), gn=gdn_norm[l].astype(F32).reshape(1, HEAD_DIM),
        ln1_g=ln1_g[l].reshape(1, d), ln1_b=ln1_b[l].reshape(1, d),
        wr=w_router[l].astype(F32).T, br=b_router[l].astype(F32).reshape(N_EXPERTS, 1),
        ln2_g=ln2_g[l].reshape(1, d), ln2_b=ln2_b[l].reshape(1, d),
    )


def kernel(x, mem, ln0_g, ln0_b, w_in, conv_qkv, conv_mix, a_log, dt_bias, gdn_norm, w_mem_kv, w_br, w_o, ln1_g, ln1_b, w_router, b_router, w_gate_up, b_gate_up, w_down, b_down, ln2_g, ln2_b):
    batch, seq_len, d = x.shape
    depth = w_in.shape[0]
    n_tok = batch * seq_len
    alpha = (2 * depth) ** 0.25
    n_assign = n_tok * TOP_K
    n_blocks = (n_assign + N_EXPERTS * (MOE_BLOCK - 1) + MOE_BLOCK - 1) // MOE_BLOCK
    assert d == D_MODEL and seq_len % TM_PRE == 0 and seq_len % TT_GDN == 0
    assert n_tok % TM_LN == 0 and n_tok % TM_POST == 0

    h = _ln_call(x.reshape(n_tok, d), ln0_g, ln0_b)
    for l in range(depth):
        w = _layer_weights(l, w_in, conv_qkv, conv_mix, a_log, dt_bias, gdn_norm, w_mem_kv, w_br,
                           w_o, ln1_g, ln1_b, w_router, b_router, ln2_g, ln2_b)
        km, vm = _memkv_call(mem, w["wkv"])
        q, k, v, gcol, grow, zs, part, g0 = _pre_call(h, km, vm, w, seq_len)
        o = _gdn_call(q, k, v, gcol, grow, batch, seq_len)
        h1, h1t, top_idx, top_w = _post_call(o, zs, g0, part, h, w, alpha)
        table, block_e = _route(top_idx, n_blocks)
        ys = _moe_call(h1t, table, block_e, l, w_gate_up, b_gate_up, w_down, b_down)
        h = _combine_call(ys, top_w.T, h1, w["ln2_g"], w["ln2_b"], alpha)
    return h.reshape(batch, seq_len, d)
```

```python
import functools

import jax
import jax.numpy as jnp
from jax import lax
from jax.experimental import pallas as pl
from jax.experimental.pallas import tpu as pltpu

F32 = jnp.float32
BF16 = jnp.bfloat16
HIGHEST = lax.Precision.HIGHEST

D_MODEL = 1024
N_HEADS = 4
HEAD_DIM = 128
WIDTH = N_HEADS * HEAD_DIM
GDN_CONV = 4
SHORT_CONV = 3
N_EXPERTS = 32
TOP_K = 4
D_FF = 1024
SWIGLU_LIMIT = 7.0
SWIGLU_ALPHA = 1.702
MOE_BLOCK = 256
N_ROWBUF = 3
LN_EPS = 1e-5
RMS_EPS = 1e-6
HALO = 8
LANES = 128
VMEM_LIMIT = 56 * 1024 * 1024

TM_PRE = 512
TM_POST = 512
TM_LN = 512
TT_GDN = 256
GDN_CHUNK = 64


def _dot(a, b, precision=None):
    return jnp.dot(a, b, preferred_element_type=F32, precision=precision)


def _dot_nt(a, b, precision=None):
    return lax.dot_general(a, b, (((1,), (1,)), ((), ())), preferred_element_type=F32,
                           precision=precision)


def _dot_tn(a, b):
    return lax.dot_general(a, b, (((0,), (0,)), ((), ())), preferred_element_type=F32)


def _layer_norm(y, g, b):
    mu = jnp.mean(y, axis=-1, keepdims=True)
    d = y - mu
    var = jnp.mean(d * d, axis=-1, keepdims=True)
    return d * lax.rsqrt(var + LN_EPS) * g + b


def _sigmoid(x):
    return 0.5 * jnp.tanh(0.5 * x) + 0.5


def _silu(x):
    return x * _sigmoid(x)


def _softplus(x):
    return jnp.maximum(x, 0.0) + jnp.log(1.0 + jnp.exp(-jnp.abs(x)))


def _const_spec(shape):
    zeros = (0,) * len(shape)
    return pl.BlockSpec(shape, lambda *_: zeros, pipeline_mode=pl.Buffered(1))


def _params(*sem):
    return pltpu.CompilerParams(dimension_semantics=sem, vmem_limit_bytes=VMEM_LIMIT)


def _ln_body(x_ref, g_ref, b_ref, o_ref):
    o_ref[...] = _layer_norm(x_ref[...], g_ref[...], b_ref[...])


def _ln_call(x, g, b):
    n, d = x.shape
    row = pl.BlockSpec((TM_LN, d), lambda i: (i, 0))
    return pl.pallas_call(
        _ln_body, out_shape=jax.ShapeDtypeStruct((n, d), F32), grid=(n // TM_LN,),
        in_specs=[row, _const_spec((1, d)), _const_spec((1, d))], out_specs=row,
        compiler_params=_params("parallel"), name="ln0")(x, g.reshape(1, d), b.reshape(1, d))


def _memkv_body(mem_ref, w_ref, k_ref, v_ref):
    kv = _dot(mem_ref[0].astype(BF16), w_ref[...])
    k_ref[0] = kv[:, :WIDTH].astype(BF16)
    v_ref[0] = kv[:, WIDTH:].astype(BF16)


def _memkv_call(mem, w_kv):
    b, m, d = mem.shape
    out = jax.ShapeDtypeStruct((b, m, WIDTH), BF16)
    blk = pl.BlockSpec((1, m, WIDTH), lambda i: (i, 0, 0))
    return pl.pallas_call(
        _memkv_body, out_shape=(out, out), grid=(b,),
        in_specs=[pl.BlockSpec((1, m, d), lambda i: (i, 0, 0)), _const_spec((d, 2 * WIDTH))],
        out_specs=(blk, blk), compiler_params=_params("parallel"), name="memkv")(mem, w_kv)


def _causal_conv(x, halo_ref, buf_ref, w_ref):
    tm = x.shape[0]
    taps = w_ref.shape[0]
    buf_ref[0:HALO, :] = halo_ref[...]
    buf_ref[HALO:HALO + tm, :] = x
    halo_ref[...] = x[tm - HALO:, :]
    acc = w_ref[taps - 1:taps, :] * x
    for j in range(taps - 1):
        off = HALO - (taps - 1) + j
        acc = acc + w_ref[j:j + 1, :] * buf_ref[off:off + tm, :]
    return acc


def _pre_body(h_ref, wq_ref, wk_ref, wv_ref, wab_ref, wabt_ref, wz_ref, wxc_ref, wgb_ref,
              wgc_ref, wqm_ref, wgl_ref, cq_ref, ck_ref, cv_ref, cmix_ref, abp_ref, abpt_ref,
              km_ref, vm_ref, wb1_ref, wb2_ref,
              q_ref, k_ref, v_ref, gcol_ref, grow_ref, zs_ref, part_ref, g0_ref,
              hq_ref, hk_ref, hv_ref, hx_ref, bufq_ref, bufk_ref, bufv_ref, bufx_ref, *,
              tiles_per_seq):
    @pl.when(pl.program_id(0) % tiles_per_seq == 0)
    def _():
        for halo_ref in (hq_ref, hk_ref, hv_ref, hx_ref):
            halo_ref[...] = jnp.zeros_like(halo_ref)

    hb = h_ref[...].astype(BF16)

    def head_l2(x, scale):
        outs = []
        for hd in range(N_HEADS):
            xh = x[:, hd * HEAD_DIM:(hd + 1) * HEAD_DIM]
            ss = jnp.sum(xh * xh, axis=-1, keepdims=True)
            outs.append(xh * (lax.rsqrt(ss + RMS_EPS) * scale))
        return jnp.concatenate(outs, axis=-1)

    proj_q = _dot(hb, wq_ref[...])
    proj_k = _dot(hb, wk_ref[...])
    q = _silu(_causal_conv(proj_q, hq_ref, bufq_ref, cq_ref))
    q_ref[...] = head_l2(q, HEAD_DIM ** -0.5)
    proj_v = _dot(hb, wv_ref[...])
    k = _silu(_causal_conv(proj_k, hk_ref, bufk_ref, ck_ref))
    k_ref[...] = head_l2(k, 1.0)
    proj_z = _dot(hb, wz_ref[...])
    v_ref[...] = _silu(_causal_conv(proj_v, hv_ref, bufv_ref, cv_ref))
    xc = _dot(hb, wxc_ref[...])
    gc = _dot(hb, wgc_ref[...])
    zs_ref[...] = _silu(proj_z)
    proj_gb = _dot(hb, wgb_ref[...])

    ab = _dot(hb, wab_ref[...])
    a_scale = abp_ref[0:1, :]
    a_bias = abp_ref[1:2, :]
    lane = lax.broadcasted_iota(jnp.int32, ab.shape, 1)
    gb_col = jnp.where(lane < N_HEADS, a_scale * _softplus(ab + a_bias), _sigmoid(ab))
    gcol_ref[...] = gb_col[:, :2 * N_HEADS]
    abt = _dot_nt(wabt_ref[...], hb)
    sub = lax.broadcasted_iota(jnp.int32, abt.shape, 0)
    grow_ref[...] = jnp.where(sub < N_HEADS,
                              abpt_ref[:, 0:1] * _softplus(abt + abpt_ref[:, 1:2]), _sigmoid(abt))

    qm = _dot(hb, wqm_ref[...]).astype(BF16)
    y_b = proj_gb * _causal_conv(gc * xc, hx_ref, bufx_ref, cmix_ref)
    gl0 = _dot(hb, wgl_ref[:, 0:D_MODEL])
    p_b = _dot(y_b.astype(BF16), wb1_ref[...])

    heads = []
    for hd in range(N_HEADS):
        sl = slice(hd * HEAD_DIM, (hd + 1) * HEAD_DIM)
        s = _dot_nt(qm[:, sl], km_ref[0, :, sl]) * (HEAD_DIM ** -0.5)
        e = jnp.exp(s - jnp.max(s, axis=-1, keepdims=True))
        p = e / jnp.sum(e, axis=-1, keepdims=True)
        heads.append(_dot(p.astype(BF16), vm_ref[0, :, sl]))
    y_m = jnp.concatenate(heads, axis=-1)
    gl1 = _dot(hb, wgl_ref[:, D_MODEL:2 * D_MODEL])
    g0_ref[...] = _sigmoid(gl0)

    p_m = _dot(y_m.astype(BF16), wb2_ref[...])
    gl2 = _dot(hb, wgl_ref[:, 2 * D_MODEL:3 * D_MODEL])
    part_ref[...] = _sigmoid(gl1) * p_b + _sigmoid(gl2) * p_m


def _pre_call(h, km, vm, w, seq_len):
    n, d = h.shape
    tm = TM_PRE
    tiles_per_seq = seq_len // tm
    row = lambda c: pl.BlockSpec((tm, c), lambda i: (i, 0))
    mem_spec = pl.BlockSpec((1,) + km.shape[1:], lambda i: (i // tiles_per_seq, 0, 0))
    consts = [w["wq"], w["wk"], w["wv"], w["wab"], w["wabt"], w["wz"], w["wxc"], w["wgb"],
              w["wgc"], w["wqm"], w["wgl"], w["cq"], w["ck"], w["cv"], w["cmix"], w["abp"],
              w["abpt"]]
    tail = [w["wb1"], w["wb2"]]
    out_shape = (
        jax.ShapeDtypeStruct((n, WIDTH), F32),
        jax.ShapeDtypeStruct((n, WIDTH), F32),
        jax.ShapeDtypeStruct((n, WIDTH), F32),
        jax.ShapeDtypeStruct((n, 2 * N_HEADS), F32),
        jax.ShapeDtypeStruct((2 * N_HEADS, n), F32),
        jax.ShapeDtypeStruct((n, WIDTH), F32),
        jax.ShapeDtypeStruct((n, d), F32),
        jax.ShapeDtypeStruct((n, d), F32),
    )
    out_specs = (row(WIDTH), row(WIDTH), row(WIDTH), row(2 * N_HEADS),
                 pl.BlockSpec((2 * N_HEADS, tm), lambda i: (0, i)), row(WIDTH), row(d), row(d))
    scratch = [pltpu.VMEM((HALO, WIDTH), F32)] * 4 + [pltpu.VMEM((HALO + tm, WIDTH), F32)] * 4
    return pl.pallas_call(
        functools.partial(_pre_body, tiles_per_seq=tiles_per_seq),
        out_shape=out_shape, grid=(n // tm,),
        in_specs=[row(d)] + [_const_spec(c.shape) for c in consts] + [mem_spec, mem_spec]
        + [_const_spec(c.shape) for c in tail],
        out_specs=out_specs, scratch_shapes=scratch,
        compiler_params=_params("arbitrary"), name="mixer_pre")(h, *consts, km, vm, *tail)


def _gdn_body(q_ref, k_ref, v_ref, gcol_ref, grow_ref, o_ref, s_ref, *, chunk):
    @pl.when(pl.program_id(1) == 0)
    def _():
        s_ref[...] = jnp.zeros_like(s_ref)

    tt = q_ref.shape[0]
    n_chunks = tt // chunk
    shift = chunk.bit_length() - 1
    row = lax.broadcasted_iota(jnp.int32, (tt, tt), 0)
    col = lax.broadcasted_iota(jnp.int32, (tt, tt), 1)
    same = (row >> shift) == (col >> shift)
    causal = jnp.logical_and(same, row >= col)
    strict = jnp.logical_and(same, row > col)
    eye = (row == col).astype(F32)
    causal_f = causal.astype(F32)
    same_f = same.astype(F32)
    n_doublings = chunk.bit_length() - 2

    gcol = gcol_ref[...]
    grow = grow_ref[...]
    gcum_col = _dot(causal_f, gcol, precision=HIGHEST)
    gcum_row = _dot_nt(grow, causal_f, precision=HIGHEST)
    gtot_col = _dot(same_f, gcol, precision=HIGHEST)
    gtot_row = _dot(grow, same_f, precision=HIGHEST)

    heads = []
    for hd in range(N_HEADS):
        lanes = slice(hd * HEAD_DIM, (hd + 1) * HEAD_DIM)
        k = k_ref[:, lanes]
        gc = gcum_col[:, hd:hd + 1]
        gr = gcum_row[hd:hd + 1, :]
        beta = gcol[:, N_HEADS + hd:N_HEADS + hd + 1]
        decay = jnp.exp(jnp.where(causal, gc - gr, -1e30))
        kb = k * beta
        k16 = k.astype(BF16)
        low = jnp.where(strict, _dot_nt(kb.astype(BF16), k16) * decay, 0.0)
        heads.append(dict(gc=gc, beta=beta, decay=decay, e_gc=jnp.exp(gc), k=k, kb=kb, k16=k16,
                          inv=eye - low, power=low))
    for _ in range(n_doublings):
        for head in heads:
            p16 = head["power"].astype(BF16)
            head["power"] = _dot(p16, p16)
        for head in heads:
            head["inv"] = head["inv"] + _dot(head["inv"].astype(BF16), head["power"].astype(BF16))
    for hd, head in enumerate(heads):
        lanes = slice(hd * HEAD_DIM, (hd + 1) * HEAD_DIM)
        q = q_ref[:, lanes]
        rhs = jnp.concatenate([v_ref[:, lanes] * head["beta"], head["kb"] * head["e_gc"]],
                              axis=-1).astype(BF16)
        uw = _dot(head["inv"].astype(BF16), rhs)
        head.update(u=uw[:, :HEAD_DIM], w=uw[:, HEAD_DIM:], qd=q * head["e_gc"],
                    ke=head["k"] * jnp.exp(gtot_col[:, hd:hd + 1] - head["gc"]),
                    qk=(_dot_nt(q.astype(BF16), head["k16"]) * head["decay"]).astype(BF16))

    pair = 2 * HEAD_DIM
    n_pairs = N_HEADS // 2
    prow = lax.broadcasted_iota(jnp.int32, (pair, pair), 0)
    pcol = lax.broadcasted_iota(jnp.int32, (pair, pair), 1)
    block_diag = (prow >= HEAD_DIM) == (pcol >= HEAD_DIM)
    first_head = lax.broadcasted_iota(jnp.int32, (1, pair), 1) < HEAD_DIM
    two = lambda p, name: jnp.concatenate([heads[2 * p][name], heads[2 * p + 1][name]], axis=-1)
    u2, w2, qd2, ke2 = ([two(p, name) for p in range(n_pairs)] for name in ("u", "w", "qd", "ke"))
    state = [s_ref[p] for p in range(n_pairs)]
    o_state = [[] for _ in range(n_pairs)]
    v_new = [[] for _ in range(n_pairs)]
    for ci in range(n_chunks):
        rows = slice(ci * chunk, (ci + 1) * chunk)
        at = slice(ci * chunk, ci * chunk + 1)
        for p in range(n_pairs):
            lhs = jnp.concatenate([w2[p][rows], qd2[p][rows]], axis=0).astype(BF16)
            res = _dot(lhs, state[p].astype(BF16))
            vn = u2[p][rows] - res[:chunk]
            o_state[p].append(res[chunk:])
            v_new[p].append(vn)
            chunk_decay = jnp.exp(jnp.where(first_head, gtot_row[2 * p:2 * p + 1, at],
                                            gtot_row[2 * p + 1:2 * p + 2, at]))
            outer = _dot_tn(ke2[p][rows].astype(BF16), vn.astype(BF16))
            state[p] = state[p] * chunk_decay + jnp.where(block_diag, outer, 0.0)
    for p in range(n_pairs):
        s_ref[p] = state[p]
        o_pair = jnp.concatenate(o_state[p], axis=0)
        v_pair = jnp.concatenate(v_new[p], axis=0).astype(BF16)
        for j in range(2):
            lanes = slice(j * HEAD_DIM, (j + 1) * HEAD_DIM)
            out_lanes = slice((2 * p + j) * HEAD_DIM, (2 * p + j + 1) * HEAD_DIM)
            o_ref[:, out_lanes] = o_pair[:, lanes] + _dot(heads[2 * p + j]["qk"], v_pair[:, lanes])


def _gdn_call(q, k, v, gcol, grow, batch, seq_len):
    n = q.shape[0]
    tt = TT_GDN
    steps = seq_len // tt
    tok = pl.BlockSpec((tt, WIDTH), lambda b, t: (b * steps + t, 0))
    return pl.pallas_call(
        functools.partial(_gdn_body, chunk=GDN_CHUNK),
        out_shape=jax.ShapeDtypeStruct((n, WIDTH), F32), grid=(batch, steps),
        in_specs=[tok, tok, tok,
                  pl.BlockSpec((tt, 2 * N_HEADS), lambda b, t: (b * steps + t, 0)),
                  pl.BlockSpec((2 * N_HEADS, tt), lambda b, t: (0, b * steps + t))],
        out_specs=tok,
        scratch_shapes=[pltpu.VMEM((N_HEADS // 2, 2 * HEAD_DIM, 2 * HEAD_DIM), F32)],
        compiler_params=_params("parallel", "arbitrary"), name="gdn")(q, k, v, gcol, grow)


def _post_body(o_ref, zs_ref, g0_ref, part_ref, h_ref, gn_ref, wb0_ref, wo_ref, lg_ref, lb_ref,
               wr_ref, br_ref, h1_ref, h1t_ref, idx_ref, tw_ref, *, alpha):
    o = o_ref[...]
    zs = zs_ref[...]
    heads = []
    for hd in range(N_HEADS):
        sl = slice(hd * HEAD_DIM, (hd + 1) * HEAD_DIM)
        oh = o[:, sl]
        ms = jnp.mean(oh * oh, axis=-1, keepdims=True)
        heads.append(oh * lax.rsqrt(ms + RMS_EPS) * gn_ref[...] * zs[:, sl])
    y_a = jnp.concatenate(heads, axis=-1).astype(BF16)
    merged = g0_ref[...] * _dot(y_a, wb0_ref[...]) + part_ref[...]
    mix = _dot(merged.astype(BF16), wo_ref[...])
    h1 = _layer_norm(alpha * h_ref[...] + mix, lg_ref[...], lb_ref[...])
    h1_ref[...] = h1
    h1t_ref[...] = h1.reshape(h1.shape[0], D_MODEL // LANES, LANES)

    logits = _dot_nt(wr_ref[...], h1, precision=HIGHEST) + br_ref[...]
    expert = lax.broadcasted_iota(jnp.int32, logits.shape, 0)
    idx, top = [], []
    for _ in range(TOP_K):
        m = jnp.max(logits, axis=0, keepdims=True)
        sel = jnp.min(jnp.where(logits == m, expert, N_EXPERTS), axis=0, keepdims=True)
        idx.append(sel)
        top.append(m)
        logits = jnp.where(expert == sel, -jnp.inf, logits)
    e = jnp.exp(jnp.concatenate(top, axis=0) - top[0])
    idx_ref[...] = jnp.concatenate(idx, axis=0)
    tw_ref[...] = e / jnp.sum(e, axis=0, keepdims=True)


def _post_call(o, zs, g0, part, h, w, alpha):
    n, d = h.shape
    tm = TM_POST
    row = lambda c: pl.BlockSpec((tm, c), lambda i: (i, 0))
    consts = [w["gn"], w["wb0"], w["wo"], w["ln1_g"], w["ln1_b"], w["wr"], w["br"]]
    return pl.pallas_call(
        functools.partial(_post_body, alpha=alpha),
        out_shape=(jax.ShapeDtypeStruct((n, d), F32),
                   jax.ShapeDtypeStruct((n, d // LANES, LANES), F32),
                   jax.ShapeDtypeStruct((TOP_K, n), jnp.int32),
                   jax.ShapeDtypeStruct((TOP_K, n), F32)),
        grid=(n // tm,),
        in_specs=[row(WIDTH), row(WIDTH), row(d), row(d), row(d)]
        + [_const_spec(c.shape) for c in consts],
        out_specs=(row(d), pl.BlockSpec((tm, d // LANES, LANES), lambda i: (i, 0, 0)),
                   pl.BlockSpec((TOP_K, tm), lambda i: (0, i)),
                   pl.BlockSpec((TOP_K, tm), lambda i: (0, i))),
        compiler_params=_params("parallel"), name="mixer_post")(o, zs, g0, part, h, *consts)


def _route(top_idx, n_blocks):
    n_tok = top_idx.shape[1]
    n_assign = n_tok * TOP_K
    n_slots = n_blocks * MOE_BLOCK
    flat_e = top_idx.T.reshape(n_assign)
    seg = 256
    onehot = (flat_e[:, None] == jnp.arange(N_EXPERTS, dtype=jnp.int32)[None, :]).astype(F32)
    oh3 = onehot.reshape(n_assign // seg, seg, N_EXPERTS)
    within = jnp.einsum("ij,bje->bie", jnp.tril(jnp.ones((seg, seg), F32)), oh3)
    seg_tot = within[:, -1, :]
    seg_off = jnp.cumsum(seg_tot, axis=0) - seg_tot
    rank = (jnp.sum(oh3 * (within + seg_off[:, None, :]), axis=-1) - 1.0).astype(jnp.int32)
    rank = rank.reshape(n_assign)
    counts = (seg_off[-1] + seg_tot[-1]).astype(jnp.int32)
    padded = (counts + MOE_BLOCK - 1) // MOE_BLOCK * MOE_BLOCK
    pad_end = jnp.cumsum(padded)
    dest = (pad_end - padded)[flat_e] + rank
    slot_a = jnp.full((n_slots,), -1, jnp.int32).at[dest].set(
        jnp.arange(n_assign, dtype=jnp.int32), unique_indices=True, mode="promise_in_bounds")
    is_pad = slot_a < 0
    pad_rank = jnp.cumsum(is_pad.astype(jnp.int32)) - 1
    a = jnp.maximum(slot_a, 0)
    slot_tok = a // TOP_K
    slot_dst = jnp.where(is_pad, n_assign + pad_rank, (a % TOP_K) * n_tok + slot_tok)
    block_start = jnp.arange(n_blocks + 1, dtype=jnp.int32) * MOE_BLOCK
    block_e = jnp.minimum(jnp.sum((pad_end[None, :] <= block_start[:, None]).astype(jnp.int32),
                                  axis=1), N_EXPERTS - 1)
    filler_tok = jnp.zeros((3 * MOE_BLOCK,), jnp.int32)
    filler_dst = n_slots + jnp.arange(MOE_BLOCK, dtype=jnp.int32)
    tok_blocks = jnp.concatenate([slot_tok.astype(jnp.int32), filler_tok]).reshape(-1, MOE_BLOCK)
    dst_blocks = jnp.concatenate([filler_dst, slot_dst.astype(jnp.int32)]).reshape(-1, MOE_BLOCK)
    steps = n_blocks + 1
    table = jnp.stack([tok_blocks[2:], dst_blocks,
                       jnp.broadcast_to(tok_blocks[0], (steps, MOE_BLOCK)),
                       jnp.broadcast_to(tok_blocks[1], (steps, MOE_BLOCK))], axis=1)
    return table, block_e


TAB_GATHER, TAB_SCATTER, TAB_FIRST, TAB_SECOND = range(4)


def _moe_body(be_ref, tab_ref, h_hbm, wgu_ref, bgu_ref, wd_ref,
              bd_ref, ys_hbm, xbuf, ybuf, wgu16, wd16, gsem, ssem, *, n_blocks):
    i = pl.program_id(0)
    cur = i % N_ROWBUF
    nxt = (i + 2) % N_ROWBUF

    def gather_row(table_row, r, slot):
        return pltpu.make_async_copy(h_hbm.at[tab_ref[0, table_row, r]], xbuf.at[slot, r],
                                     gsem.at[slot])

    def gather_wait(slot):
        pltpu.make_async_copy(h_hbm.at[pl.ds(0, MOE_BLOCK)], xbuf.at[slot],
                              gsem.at[slot]).wait()

    def scatter_wait(slot):
        pltpu.make_async_copy(ybuf.at[slot], ys_hbm.at[pl.ds(0, MOE_BLOCK)],
                              ssem.at[slot]).wait()

    @pl.when(i == 0)
    def _():
        ybuf[...] = jnp.zeros_like(ybuf)
        for r in range(MOE_BLOCK):
            gather_row(TAB_FIRST, r, 0).start()
        for r in range(MOE_BLOCK):
            gather_row(TAB_SECOND, r, 1).start()

    @pl.when(jnp.logical_or(i == 0, be_ref[i] != be_ref[jnp.maximum(i - 1, 0)]))
    def _():
        wgu16[...] = wgu_ref[0, 0].astype(BF16)
        wd16[...] = wd_ref[0, 0].astype(BF16)

    gather_wait(cur)

    @pl.when(i >= 2)
    def _():
        scatter_wait(cur)

    x16 = xbuf[cur].reshape(MOE_BLOCK, D_MODEL).astype(BF16)
    for r in range(MOE_BLOCK):
        pltpu.make_async_copy(ybuf.at[nxt, r], ys_hbm.at[tab_ref[0, TAB_SCATTER, r]], ssem.at[nxt]).start()
    for r in range(MOE_BLOCK):
        gather_row(TAB_GATHER, r, nxt).start()
    gu = _dot(x16, wgu16[...]) + bgu_ref[0, 0]
    gate = jnp.minimum(gu[:, :D_FF], SWIGLU_LIMIT)
    up = jnp.clip(gu[:, D_FF:], -SWIGLU_LIMIT, SWIGLU_LIMIT)
    act = (up + 1.0) * (gate * _sigmoid(gate * SWIGLU_ALPHA))
    y = _dot(act.astype(BF16), wd16[...]) + bd_ref[0, 0]
    ybuf[cur] = y.reshape(MOE_BLOCK, D_MODEL // LANES, LANES)

    @pl.when(i == n_blocks)
    def _():
        gather_wait((i + 1) % N_ROWBUF)
        gather_wait(nxt)
        scatter_wait((i + 1) % N_ROWBUF)
        scatter_wait(nxt)


def _moe_call(h1t, table, block_e, layer, w_gate_up, b_gate_up, w_down, b_down):
    n, sub, lanes = h1t.shape
    d = sub * lanes
    n_blocks = block_e.shape[0] - 1
    per_expert = lambda r, c: pl.BlockSpec((1, 1, r, c), lambda i, be: (layer, be[i], 0, 0))
    shape4 = lambda b: b.reshape(b.shape[0], N_EXPERTS, 1, b.shape[-1])
    grid_spec = pltpu.PrefetchScalarGridSpec(
        num_scalar_prefetch=1, grid=(n_blocks + 1,),
        in_specs=[pl.BlockSpec((1,) + table.shape[1:], lambda i, be: (i, 0, 0),
                               memory_space=pltpu.SMEM),
                  pl.BlockSpec(memory_space=pl.ANY),
                  per_expert(d, 2 * D_FF), per_expert(1, 2 * D_FF),
                  per_expert(D_FF, d), per_expert(1, d)],
        out_specs=pl.BlockSpec(memory_space=pl.ANY),
        scratch_shapes=[pltpu.VMEM((N_ROWBUF, MOE_BLOCK, sub, lanes), F32),
                        pltpu.VMEM((N_ROWBUF, MOE_BLOCK, sub, lanes), F32),
                        pltpu.VMEM((d, 2 * D_FF), BF16), pltpu.VMEM((D_FF, d), BF16),
                        pltpu.SemaphoreType.DMA((N_ROWBUF,)),
                        pltpu.SemaphoreType.DMA((N_ROWBUF,))])
    return pl.pallas_call(
        functools.partial(_moe_body, n_blocks=n_blocks),
        out_shape=jax.ShapeDtypeStruct(((n_blocks + 1) * MOE_BLOCK, sub, lanes), F32),
        grid_spec=grid_spec, compiler_params=_params("arbitrary"), name="moe")(
            block_e, table, h1t,
            w_gate_up, shape4(b_gate_up), w_down, shape4(b_down))


def _combine_body(y0_ref, y1_ref, y2_ref, y3_ref, tw_ref, h_ref, g_ref, b_ref, o_ref, *, alpha):
    tw = tw_ref[...]
    ffn = None
    for j, y_ref in enumerate((y0_ref, y1_ref, y2_ref, y3_ref)):
        term = tw[:, j:j + 1] * y_ref[...].reshape(h_ref.shape)
        ffn = term if ffn is None else ffn + term
    o_ref[...] = _layer_norm(alpha * h_ref[...] + ffn, g_ref[...], b_ref[...])


def _combine_call(ys, top_w, h1, g, b, alpha):
    n, d = h1.shape
    tm = TM_POST
    tiles = n // tm
    row = pl.BlockSpec((tm, d), lambda i: (i, 0))
    y_specs = [pl.BlockSpec((tm,) + ys.shape[1:], lambda i, j=j: (j * tiles + i, 0, 0))
               for j in range(TOP_K)]
    return pl.pallas_call(
        functools.partial(_combine_body, alpha=alpha),
        out_shape=jax.ShapeDtypeStruct((n, d), F32), grid=(tiles,),
        in_specs=y_specs + [pl.BlockSpec((tm, TOP_K), lambda i: (i, 0)), row,
                            _const_spec((1, d)), _const_spec((1, d))],
        out_specs=row, compiler_params=_params("parallel"), name="combine")(
            ys, ys, ys, ys, top_w, h1, g, b)


def _layer_weights(l, w_in, conv_qkv, conv_mix, a_log, dt_bias, gdn_norm, w_mem_kv, w_br, w_o,
                   ln1_g, ln1_b, w_router, b_router, ln2_g, ln2_b):
    d = D_MODEL
    wi = w_in[l].astype(BF16)
    c0 = 3 * WIDTH
    c_z = c0 + 2 * N_HEADS
    c_x = c_z + WIDTH
    c_gl = c_x + 4 * WIDTH
    wab = jnp.zeros((d, LANES), BF16).at[:, :2 * N_HEADS].set(wi[:, c0:c_z])
    neg_a = -jnp.exp(a_log[l].astype(F32))
    dtb = dt_bias[l].astype(F32)
    abp = jnp.zeros((2, LANES), F32).at[0, :N_HEADS].set(neg_a).at[1, :N_HEADS].set(dtb)
    abpt = jnp.zeros((2 * N_HEADS, 2), F32).at[:N_HEADS, 0].set(neg_a).at[:N_HEADS, 1].set(dtb)
    cq = conv_qkv[l].astype(F32)
    return dict(
        wq=wi[:, 0:WIDTH], wk=wi[:, WIDTH:2 * WIDTH], wv=wi[:, 2 * WIDTH:c0],
        wab=wab, wabt=wi[:, c0:c_z].T, wz=wi[:, c_z:c_x],
        wxc=wi[:, c_x:c_x + WIDTH], wgb=wi[:, c_x + WIDTH:c_x + 2 * WIDTH],
        wgc=wi[:, c_x + 2 * WIDTH:c_x + 3 * WIDTH], wqm=wi[:, c_x + 3 * WIDTH:c_gl],
        wgl=wi[:, c_gl:],
        cq=cq[:, 0:WIDTH], ck=cq[:, WIDTH:2 * WIDTH], cv=cq[:, 2 * WIDTH:],
        cmix=conv_mix[l].astype(F32), abp=abp, abpt=abpt,
        wkv=w_mem_kv[l].astype(BF16),
        wb0=w_br[l, 0].astype(BF16), wb1=w_br[l, 1].astype(BF16), wb2=w_br[l, 2].astype(BF16),
        wo=w_o[l].astype(BF16), gn=gdn_norm[l].astype(F32).reshape(1, HEAD_DIM),
        ln1_g=ln1_g[l].reshape(1, d), ln1_b=ln1_b[l].reshape(1, d),
        wr=w_router[l].astype(F32).T, br=b_router[l].astype(F32).reshape(N_EXPERTS, 1),
        ln2_g=ln2_g[l].reshape(1, d), ln2_b=ln2_b[l].reshape(1, d),
    )


def kernel(x, mem, ln0_g, ln0_b, w_in, conv_qkv, conv_mix, a_log, dt_bias, gdn_norm, w_mem_kv, w_br, w_o, ln1_g, ln1_b, w_router, b_router, w_gate_up, b_gate_up, w_down, b_down, ln2_g, ln2_b):
    batch, seq_len, d = x.shape
    depth = w_in.shape[0]
    n_tok = batch * seq_len
    alpha = (2 * depth) ** 0.25
    n_assign = n_tok * TOP_K
    n_blocks = (n_assign + N_EXPERTS * (MOE_BLOCK - 1) + MOE_BLOCK - 1) // MOE_BLOCK
    assert d == D_MODEL and seq_len % TM_PRE == 0 and seq_len % TT_GDN == 0
    assert n_tok % TM_LN == 0 and n_tok % TM_POST == 0

    h = _ln_call(x.reshape(n_tok, d), ln0_g, ln0_b)
    for l in range(depth):
        w = _layer_weights(l, w_in, conv_qkv, conv_mix, a_log, dt_bias, gdn_norm, w_mem_kv, w_br,
                           w_o, ln1_g, ln1_b, w_router, b_router, ln2_g, ln2_b)
        km, vm = _memkv_call(mem, w["wkv"])
        q, k, v, gcol, grow, zs, part, g0 = _pre_call(h, km, vm, w, seq_len)
        o = _gdn_call(q, k, v, gcol, grow, batch, seq_len)
        h1, h1t, top_idx, top_w = _post_call(o, zs, g0, part, h, w, alpha)
        table, block_e = _route(top_idx, n_blocks)
        ys = _moe_call(h1t, table, block_e, l, w_gate_up, b_gate_up, w_down, b_down)
        h = _combine_call(ys, top_w.T, h1, w["ln2_g"], w["ln2_b"], alpha)
    return h.reshape(batch, seq_len, d)
```
